```python
import jax, jax.numpy as jnp
from jax import lax
import numpy as np

D_MODEL = 1024
BATCH = 4
SEQ = 8192
DEPTH = 2

GM_WIDTH = D_MODEL
GM_GROUPS = 8
GM_GROUP_DIM = GM_WIDTH // GM_GROUPS
GM_CHUNK = 128
SSM_D_INNER = 2 * D_MODEL
SSM_HEAD_DIM = 64
SSM_HEADS = SSM_D_INNER // SSM_HEAD_DIM
SSM_GROUPS = 8
SSM_HEADS_PER_GROUP = SSM_HEADS // SSM_GROUPS
SSM_STATE = 128
SSM_CONV = 4
SSM_CHUNK = 128
SSM_CONV_DIM = SSM_D_INNER + 2 * SSM_GROUPS * SSM_STATE
N_BRANCH = 2
SPLITS = (
    GM_WIDTH,
    2 * GM_WIDTH,
    3 * GM_WIDTH,
    3 * GM_WIDTH + SSM_D_INNER,
    3 * GM_WIDTH + SSM_D_INNER + SSM_CONV_DIM,
    3 * GM_WIDTH + SSM_D_INNER + SSM_CONV_DIM + SSM_HEADS,
    3 * GM_WIDTH + SSM_D_INNER + SSM_CONV_DIM + SSM_HEADS + D_MODEL,
)
N_IN = 3 * GM_WIDTH + SSM_D_INNER + SSM_CONV_DIM + SSM_HEADS + N_BRANCH * D_MODEL
EPS = 1e-6

kernel_name = "hybrid_gmlp_ssd_gated_merge"


def rms_norm(x, w):
    xf = x.astype(jnp.float32)
    y = xf * lax.rsqrt(jnp.mean(xf * xf, axis=-1, keepdims=True) + EPS)
    return (y * w.astype(jnp.float32)).astype(x.dtype)


def layer_norm(x, w, b):
    xf = x.astype(jnp.float32)
    mu = jnp.mean(xf, axis=-1, keepdims=True)
    var = jnp.mean(jnp.square(xf - mu), axis=-1, keepdims=True)
    y = (xf - mu) * lax.rsqrt(var + EPS)
    return (y * w.astype(jnp.float32) + b.astype(jnp.float32)).astype(x.dtype)


def spatial_gating(u, v, ln_w, ln_b, w_s, b_s):
    b, s, _ = v.shape
    v = layer_norm(v, ln_w, ln_b)
    v = v.reshape(b, s // GM_CHUNK, GM_CHUNK, GM_GROUPS, GM_GROUP_DIM)
    mask = jnp.tril(jnp.ones((GM_CHUNK, GM_CHUNK), dtype=bool))
    w = jnp.where(mask[None], w_s, jnp.zeros_like(w_s))
    mixed = jnp.einsum('gts,bnsgd->bntgd', w, v) + b_s.T[None, None, :, :, None]
    return u * mixed.reshape(b, s, GM_WIDTH)


def causal_depthwise_conv(x, w, bias):
    k, ch = w.shape
    out = lax.conv_general_dilated(
        x, w[:, None, :], window_strides=(1,), padding=[(k - 1, 0)],
        dimension_numbers=('NWC', 'WIO', 'NWC'), feature_group_count=ch)
    return out + bias


def segsum_exp(a):
    t = a.shape[-1]
    cs = jnp.cumsum(a, axis=-1)
    diff = cs[..., :, None] - cs[..., None, :]
    mask = jnp.tril(jnp.ones((t, t), dtype=bool))
    return jnp.exp(jnp.where(mask, diff, -jnp.inf))


def ssd_scan(x, dt, a, bmat, cmat):
    b, s, h, p = x.shape
    q = SSM_CHUNK
    nc = s // q
    g, r, n = SSM_GROUPS, SSM_HEADS_PER_GROUP, SSM_STATE
    xd = (x * dt[..., None]).reshape(b, nc, q, g, r, p)
    adt = jnp.moveaxis((dt * a).astype(jnp.float32).reshape(b, nc, q, g, r), 2, -1)
    a_cs = jnp.cumsum(adt, axis=-1)
    bc = bmat.reshape(b, nc, q, g, n)
    cc = cmat.reshape(b, nc, q, g, n)
    decay = segsum_exp(adt)
    cb = jnp.einsum('bclgn,bcsgn->bcgls', cc, bc)
    y_diag = jnp.einsum('bcgls,bcgrls,bcsgrp->bclgrp', cb, decay, xd)
    decay_to_end = jnp.exp(a_cs[..., -1:] - a_cs)
    chunk_states = jnp.einsum('bcsgn,bcgrs,bcsgrp->bcgrpn', bc, decay_to_end, xd).astype(jnp.float32)
    chunk_decay = jnp.exp(a_cs[..., -1])

    def step(state, inp):
        cs, cd = inp
        return state * cd[..., None, None] + cs, state

    init = jnp.zeros((b, g, r, p, n), jnp.float32)
    _, prev_states = lax.scan(step, init, (jnp.moveaxis(chunk_states, 1, 0), jnp.moveaxis(chunk_decay, 1, 0)))
    prev_states = jnp.moveaxis(prev_states, 0, 1)
    y_off = jnp.einsum('bclgn,bcgrpn,bcgrl->bclgrp', cc, prev_states, jnp.exp(a_cs))
    return (y_diag + y_off).reshape(b, s, h, p).astype(x.dtype)


def hybrid_layer(x, c, ada_w, ada_b, norm_w, w_in, gm_ln_w, gm_ln_b, gm_ws, gm_bs,
                 conv_w, conv_b, dt_bias, a_log, d_skip, ssm_norm_w, w_proj_a, w_proj_b, w_out):
    b, s, _ = x.shape
    mod = jax.nn.silu(c) @ ada_w + ada_b
    shift, scale, gate = jnp.split(mod, 3, axis=-1)
    h = rms_norm(x, norm_w) * (1 + scale[:, None, :]) + shift[:, None, :]
    proj = h @ w_in
    gm_u, gm_v, gm_z, ssm_z, xbc, dt_raw, g_a, g_b = jnp.split(proj, SPLITS, axis=-1)
    y_a = spatial_gating(jax.nn.gelu(gm_u), jax.nn.gelu(gm_v), gm_ln_w, gm_ln_b, gm_ws, gm_bs) * jax.nn.silu(gm_z)
    xbc = jax.nn.silu(causal_depthwise_conv(xbc, conv_w, conv_b))
    xs, bm, cm = jnp.split(xbc, (SSM_D_INNER, SSM_D_INNER + SSM_GROUPS * SSM_STATE), axis=-1)
    dt = jax.nn.softplus((dt_raw + dt_bias).astype(jnp.float32))
    a = -jnp.exp(a_log.astype(jnp.float32))
    xh = xs.reshape(b, s, SSM_HEADS, SSM_HEAD_DIM)
    y_b = ssd_scan(xh, dt, a, bm.reshape(b, s, SSM_GROUPS, SSM_STATE), cm.reshape(b, s, SSM_GROUPS, SSM_STATE))
    y_b = y_b + xh * d_skip[:, None]
    yz = (y_b.reshape(b, s, SSM_D_INNER) * jax.nn.silu(ssm_z)).reshape(b, s, SSM_GROUPS, SSM_D_INNER // SSM_GROUPS)
    y_b = rms_norm(yz, ssm_norm_w.reshape(SSM_GROUPS, -1)).reshape(b, s, SSM_D_INNER)
    merged = jax.nn.sigmoid(g_a) * (y_a @ w_proj_a) + jax.nn.sigmoid(g_b) * (y_b @ w_proj_b)
    return x + gate[:, None, :] * (merged @ w_out)


def setup_inputs(seed: int = 0) -> dict:
    key = jax.random.key(seed)
    ks = jax.random.split(key, 24)
    nrm = jax.random.normal
    L, D = DEPTH, D_MODEL
    dt0 = jnp.exp(jax.random.uniform(ks[10], (L, SSM_HEADS), minval=np.log(1e-3), maxval=np.log(1e-1)))
    return {
        "x": nrm(ks[0], (BATCH, SEQ, D), jnp.float32),
        "c": nrm(ks[1], (BATCH, D), jnp.float32),
        "ada_w": nrm(ks[2], (L, D, 3 * D), jnp.float32) * D ** -0.5,
        "ada_b": 0.01 * nrm(ks[3], (L, 3 * D), jnp.float32),
        "norm_w": 1.0 + 0.1 * nrm(ks[4], (L, D), jnp.float32),
        "w_in": nrm(ks[5], (L, D, N_IN), jnp.float32) * D ** -0.5,
        "gm_ln_w": 1.0 + 0.1 * nrm(ks[6], (L, GM_WIDTH), jnp.float32),
        "gm_ln_b": 0.01 * nrm(ks[7], (L, GM_WIDTH), jnp.float32),
        "gm_ws": nrm(ks[8], (L, GM_GROUPS, GM_CHUNK, GM_CHUNK), jnp.float32) * GM_CHUNK ** -0.5,
        "gm_bs": 1.0 + 0.1 * nrm(ks[9], (L, GM_GROUPS, GM_CHUNK), jnp.float32),
        "conv_w": nrm(ks[11], (L, SSM_CONV, SSM_CONV_DIM), jnp.float32) * SSM_CONV ** -0.5,
        "conv_b": 0.01 * nrm(ks[12], (L, SSM_CONV_DIM), jnp.float32),
        "dt_bias": dt0 + jnp.log(-jnp.expm1(-dt0)),
        "a_log": jnp.log(jax.random.uniform(ks[13], (L, SSM_HEADS), minval=1.0, maxval=16.0)),
        "d_skip": 1.0 + 0.1 * nrm(ks[14], (L, SSM_HEADS), jnp.float32),
        "ssm_norm_w": 1.0 + 0.1 * nrm(ks[15], (L, SSM_D_INNER), jnp.float32),
        "w_proj_a": nrm(ks[16], (L, GM_WIDTH, D), jnp.float32) * GM_WIDTH ** -0.5,
        "w_proj_b": nrm(ks[17], (L, SSM_D_INNER, D), jnp.float32) * SSM_D_INNER ** -0.5,
        "w_out": nrm(ks[18], (L, D, D), jnp.float32) * D ** -0.5,
        "final_norm_w": 1.0 + 0.1 * nrm(ks[19], (D,), jnp.float32),
    }


def reference(x, c, ada_w, ada_b, norm_w, w_in, gm_ln_w, gm_ln_b, gm_ws, gm_bs, conv_w, conv_b,
              dt_bias, a_log, d_skip, ssm_norm_w, w_proj_a, w_proj_b, w_out, final_norm_w):
    for i in range(DEPTH):
        x = hybrid_layer(x, c, ada_w[i], ada_b[i], norm_w[i], w_in[i], gm_ln_w[i], gm_ln_b[i],
                         gm_ws[i], gm_bs[i], conv_w[i], conv_b[i], dt_bias[i], a_log[i], d_skip[i],
                         ssm_norm_w[i], w_proj_a[i], w_proj_b[i], w_out[i])
    return rms_norm(x, final_norm_w)
```

```python
import functools

import jax
import jax.numpy as jnp
from jax import lax
from jax.experimental import pallas as pl
from jax.experimental.pallas import tpu as pltpu

F32 = jnp.float32
BF16 = jnp.bfloat16

D_MODEL = 1024
GM_WIDTH = 1024
GM_GROUPS = 8
CHUNK = 128
SSM_D_INNER = 2048
SSM_HEAD_DIM = 64
SSM_HEADS = 32
SSM_GROUPS = 8
HEADS_PER_GROUP = 4
GROUP_WIDTH = SSM_D_INNER // SSM_GROUPS
SSM_STATE = 128
SSM_CONV = 4
CONV_DIM = SSM_D_INNER + 2 * SSM_GROUPS * SSM_STATE
EPS = 1e-6

COL_U = 0
COL_V = COL_U + GM_WIDTH
COL_Z = COL_V + GM_WIDTH
COL_SZ = COL_Z + GM_WIDTH
COL_XBC = COL_SZ + SSM_D_INNER
COL_GA = COL_XBC + CONV_DIM
COL_GB = COL_GA + D_MODEL
N_MAIN = COL_GB + D_MODEL
DT_COL0 = 3 * GM_WIDTH + SSM_D_INNER + CONV_DIM
LANES = 128
TAIL_ROWS = 16

VMEM_LIMIT_BYTES = 56 * 1024 * 1024


def _sigmoid(x):
    return 0.5 * jnp.tanh(0.5 * x) + 0.5


def _silu(x):
    return x * _sigmoid(x)


def _gelu_tanh(x):
    c = 0.7978845608028654
    return x * (0.5 * (1.0 + jnp.tanh(c * (x + 0.044715 * (x * x * x)))))


def _split3(v):
    hi = v.astype(BF16)
    r1 = v - hi.astype(F32)
    mid = r1.astype(BF16)
    lo = (r1 - mid.astype(F32)).astype(BF16)
    return hi, mid, lo


def _split2(v):
    hi = v.astype(BF16)
    lo = (v - hi.astype(F32)).astype(BF16)
    return hi, lo


def _adaln_kernel(c_ref, w_ref, b_ref, o_ref):
    s = _silu(c_ref[...])
    w = w_ref[0]
    s_hi, s_lo = _split2(s)
    w_hi, w_lo = _split2(w)
    acc = jnp.dot(s_hi, w_hi, preferred_element_type=F32)
    acc = acc + jnp.dot(s_lo, w_hi, preferred_element_type=F32)
    acc = acc + jnp.dot(s_hi, w_lo, preferred_element_type=F32)
    o_ref[0] = acc + b_ref[0]


def _adaln_mod(c, ada_w, ada_b):
    depth, d, n3 = ada_w.shape
    b = c.shape[0]
    tn = 1024
    return pl.pallas_call(
        _adaln_kernel,
        out_shape=jax.ShapeDtypeStruct((depth, b, n3), F32),
        grid=(depth, n3 // tn),
        in_specs=[
            pl.BlockSpec((b, d), lambda l, n: (0, 0)),
            pl.BlockSpec((1, d, tn), lambda l, n: (l, 0, n)),
            pl.BlockSpec((1, 1, tn), lambda l, n: (l, 0, n)),
        ],
        out_specs=pl.BlockSpec((1, b, tn), lambda l, n: (l, 0, n)),
        compiler_params=pltpu.CompilerParams(dimension_semantics=("arbitrary", "arbitrary")),
        name="adaln_mod",
    )(c, ada_w, ada_b.reshape(depth, 1, n3))


def _inproj_kernel(x_ref, nw_ref, scale_ref, shift_ref, w_ref, wdt_ref, o_ref, odt_ref, h_ref):
    @pl.when(pl.program_id(2) == 0)
    def _():
        x = x_ref[0]
        y = x * lax.rsqrt(jnp.mean(x * x, axis=-1, keepdims=True) + EPS)
        h = (y * nw_ref[...]) * (1.0 + scale_ref[0]) + shift_ref[0]
        hb = h.astype(BF16)
        h_ref[...] = hb
        odt_ref[0] = jnp.dot(hb, wdt_ref[...], preferred_element_type=F32)

    o_ref[0] = jnp.dot(h_ref[...], w_ref[...], preferred_element_type=F32).astype(BF16)


def _in_proj(x, norm_w, scale, shift, w_main, w_dt, *, tm, tn):
    b, s, d = x.shape
    n = w_main.shape[1]
    return pl.pallas_call(
        _inproj_kernel,
        out_shape=(jax.ShapeDtypeStruct((b, s, n), BF16),
                   jax.ShapeDtypeStruct((b, s, LANES), F32)),
        grid=(b, s // tm, n // tn),
        in_specs=[
            pl.BlockSpec((1, tm, d), lambda i, m, j: (i, m, 0)),
            pl.BlockSpec((1, d), lambda i, m, j: (0, 0)),
            pl.BlockSpec((1, 1, d), lambda i, m, j: (i, 0, 0)),
            pl.BlockSpec((1, 1, d), lambda i, m, j: (i, 0, 0)),
            pl.BlockSpec((d, tn), lambda i, m, j: (0, j)),
            pl.BlockSpec((d, LANES), lambda i, m, j: (0, 0)),
        ],
        out_specs=(pl.BlockSpec((1, tm, tn), lambda i, m, j: (i, m, j)),
                   pl.BlockSpec((1, tm, LANES), lambda i, m, j: (i, m, 0))),
        scratch_shapes=[pltpu.VMEM((tm, d), BF16)],
        compiler_params=pltpu.CompilerParams(
            dimension_semantics=("arbitrary", "arbitrary", "arbitrary"),
            vmem_limit_bytes=VMEM_LIMIT_BYTES),
        name="in_proj",
    )(x, norm_w.reshape(1, d), scale, shift, w_main, w_dt)


def _mixer_kernel(p_ref, dt_ref, x_ref, gate_ref, lnw_ref, lnb_ref, ws_ref, bsf_ref, convw_ref, convb_ref,
                  dtb_ref, alog_ref, dskip_ref, nrmw_ref, wpa_ref, wpb_ref, wo_ref, e2_ref, fnw_ref,
                  o_ref, state_ref, tail_ref, ya_ref, yb_ref, *, tq, final_norm):
    @pl.when(pl.program_id(1) == 0)
    def _():
        state_ref[...] = jnp.zeros_like(state_ref)
        tail_ref[...] = jnp.zeros_like(tail_ref)

    row = lax.broadcasted_iota(jnp.int32, (CHUNK, CHUNK), 0)
    col = lax.broadcasted_iota(jnp.int32, (CHUNK, CHUNK), 1)
    tril = row >= col
    tri_ones = jnp.where(tril, 1.0, 0.0).astype(BF16)
    srow = lax.broadcasted_iota(jnp.int32, (CHUNK, TAIL_ROWS + CHUNK), 0)
    scol = lax.broadcasted_iota(jnp.int32, (CHUNK, TAIL_ROWS + CHUNK), 1)
    shifts = [jnp.where(scol == srow + (TAIL_ROWS - j), 1.0, 0.0).astype(BF16) for j in range(1, SSM_CONV)]
    lane_blk = lax.broadcasted_iota(jnp.int32, (CHUNK, GROUP_WIDTH), 1) // SSM_HEAD_DIM

    ws = [jnp.where(tril, ws_ref[g], 0.0).astype(BF16) for g in range(GM_GROUPS)]
    a_neg = -jnp.exp(alog_ref[...])
    convw = convw_ref[...]

    def chunk_body(c, carry):
        r0 = pl.multiple_of(c * CHUNK, CHUNK)
        rows = pl.ds(r0, CHUNK)

        u = _gelu_tanh(p_ref[0, rows, COL_U:COL_U + GM_WIDTH].astype(F32))
        v = _gelu_tanh(p_ref[0, rows, COL_V:COL_V + GM_WIDTH].astype(F32))
        mu = jnp.mean(v, axis=-1, keepdims=True)
        vc = v - mu
        var = jnp.mean(vc * vc, axis=-1, keepdims=True)
        vn = ((vc * lax.rsqrt(var + EPS)) * lnw_ref[...] + lnb_ref[...]).astype(BF16)
        mixed = jnp.concatenate(
            [jnp.dot(ws[g], vn[:, g * CHUNK:(g + 1) * CHUNK], preferred_element_type=F32)
             for g in range(GM_GROUPS)], axis=1) + bsf_ref[...]
        z = p_ref[0, rows, COL_Z:COL_Z + GM_WIDTH].astype(F32)
        ya_ref[rows, :] = ((u * mixed) * _silu(z)).astype(BF16)

        xraw = p_ref[0, rows, COL_XBC:COL_XBC + CONV_DIM]
        xe = jnp.concatenate([tail_ref[...], xraw], axis=0)
        tail_ref[...] = xraw[CHUNK - TAIL_ROWS:, :]
        acc = xraw.astype(F32) * convw[SSM_CONV - 1:SSM_CONV, :] + convb_ref[...]
        for j in range(1, SSM_CONV):
            sh = jnp.dot(shifts[j - 1], xe, preferred_element_type=F32)
            acc = acc + sh * convw[SSM_CONV - 1 - j:SSM_CONV - j, :]
        xbc = _silu(acc)
        xs = xbc[:, :SSM_D_INNER]
        xs_b = xs.astype(BF16)
        bm_b = xbc[:, SSM_D_INNER:SSM_D_INNER + SSM_GROUPS * SSM_STATE].astype(BF16)
        cm_b = xbc[:, SSM_D_INNER + SSM_GROUPS * SSM_STATE:].astype(BF16)

        dtv = dt_ref[0, rows, :] + dtb_ref[...]
        dt = jnp.maximum(dtv, 0.0) + jnp.log(1.0 + jnp.exp(-jnp.abs(dtv)))
        adt = dt * a_neg
        hi, mid, lo = _split3(adt)
        cs3 = jnp.dot(tri_ones, jnp.concatenate([hi, mid, lo], axis=1), preferred_element_type=F32)
        a_cs = cs3[:, :LANES] + cs3[:, LANES:2 * LANES] + cs3[:, 2 * LANES:]
        a_last = a_cs[CHUNK - 1:CHUNK, :]
        a_cs_t = a_cs.T
        dt_t = dt.T
        ea = jnp.exp(a_cs)
        w_end = dt * jnp.exp(a_last - a_cs)
        ea_hi, ea_lo = _split2(ea)
        we_hi, we_lo = _split2(w_end)
        ea_x = jnp.dot(jnp.concatenate([ea_hi, ea_lo], axis=1), e2_ref[...], preferred_element_type=F32)
        we_x = jnp.dot(jnp.concatenate([we_hi, we_lo], axis=1), e2_ref[...], preferred_element_type=F32)
        cd_x = ea_x[CHUNK - 1:CHUNK, :]

        for g in range(SSM_GROUPS):
            gs = slice(g * GROUP_WIDTH, (g + 1) * GROUP_WIDTH)
            ns = slice(g * SSM_STATE, (g + 1) * SSM_STATE)
            bm_g = bm_b[:, ns]
            cm_g = cm_b[:, ns]
            xs_g = xs[:, gs]
            xs_gb = xs_b[:, gs]
            cb = lax.dot_general(cm_g, bm_g, (((1,), (1,)), ((), ())), preferred_element_type=F32)
            ms = []
            for r in range(HEADS_PER_GROUP):
                h = g * HEADS_PER_GROUP + r
                diff = a_cs[:, h:h + 1] - a_cs_t[h:h + 1, :]
                decay = jnp.where(tril, jnp.exp(diff), 0.0)
                ms.append(((cb * decay) * dt_t[h:h + 1, :]).astype(BF16))
            lhs = jnp.concatenate(ms, axis=1)
            rhs = jnp.concatenate(
                [jnp.where(lane_blk == r, xs_gb, jnp.zeros_like(xs_gb)) for r in range(HEADS_PER_GROUP)],
                axis=0)
            y_diag = jnp.dot(lhs, rhs, preferred_element_type=F32)
            st = state_ref[:, gs]
            y_off = jnp.dot(cm_g, st.astype(BF16), preferred_element_type=F32) * ea_x[:, gs]
            xw = (xs_g * we_x[:, gs]).astype(BF16)
            new_st = lax.dot_general(bm_g, xw, (((0,), (0,)), ((), ())), preferred_element_type=F32)
            state_ref[:, gs] = st * cd_x[:, gs] + new_st
            yb = y_diag + y_off + xs_g * dskip_ref[:, gs]
            sz = p_ref[0, rows, COL_SZ + g * GROUP_WIDTH:COL_SZ + (g + 1) * GROUP_WIDTH].astype(F32)
            yz = yb * _silu(sz)
            inv = lax.rsqrt(jnp.mean(yz * yz, axis=-1, keepdims=True) + EPS)
            yb_ref[rows, gs] = ((yz * inv) * nrmw_ref[:, gs]).astype(BF16)
        return carry

    lax.fori_loop(0, tq // CHUNK, chunk_body, 0)

    pa = jnp.dot(ya_ref[...], wpa_ref[...], preferred_element_type=F32)
    pb = jnp.dot(yb_ref[...], wpb_ref[...], preferred_element_type=F32)
    ga = _sigmoid(p_ref[0, :, COL_GA:COL_GA + D_MODEL].astype(F32))
    gb = _sigmoid(p_ref[0, :, COL_GB:COL_GB + D_MODEL].astype(F32))
    merged = (ga * pa + gb * pb).astype(BF16)
    out = x_ref[0] + gate_ref[0] * jnp.dot(merged, wo_ref[...], preferred_element_type=F32)
    if final_norm:
        out = (out * lax.rsqrt(jnp.mean(out * out, axis=-1, keepdims=True) + EPS)) * fnw_ref[...]
    o_ref[0] = out


def _mixer(proj, dt_raw, x, gate, lp, e2, fnw, *, tq, final_norm):
    b, s, d = x.shape
    const2 = lambda i, j: (0, 0)
    const3 = lambda i, j: (0, 0, 0)

    def resident(shape):
        idx = const2 if len(shape) == 2 else const3
        return pl.BlockSpec(shape, idx, pipeline_mode=pl.Buffered(1))

    in_specs = [
        pl.BlockSpec((1, tq, N_MAIN), lambda i, j: (i, j, 0)),
        pl.BlockSpec((1, tq, LANES), lambda i, j: (i, j, 0)),
        pl.BlockSpec((1, tq, d), lambda i, j: (i, j, 0)),
        pl.BlockSpec((1, 1, d), lambda i, j: (i, 0, 0)),
        resident((1, GM_WIDTH)), resident((1, GM_WIDTH)),
        resident((GM_GROUPS, CHUNK, CHUNK)), resident((CHUNK, GM_WIDTH)),
        resident((SSM_CONV, CONV_DIM)), resident((1, CONV_DIM)),
        resident((1, LANES)), resident((1, LANES)),
        resident((1, SSM_D_INNER)), resident((1, SSM_D_INNER)),
        resident((GM_WIDTH, d)), resident((SSM_D_INNER, d)), resident((d, d)),
        resident((2 * LANES, SSM_D_INNER)), resident((1, d)),
    ]
    return pl.pallas_call(
        functools.partial(_mixer_kernel, tq=tq, final_norm=final_norm),
        out_shape=jax.ShapeDtypeStruct((b, s, d), F32),
        grid=(b, s // tq),
        in_specs=in_specs,
        out_specs=pl.BlockSpec((1, tq, d), lambda i, j: (i, j, 0)),
        scratch_shapes=[
            pltpu.VMEM((SSM_STATE, SSM_D_INNER), F32),
            pltpu.VMEM((TAIL_ROWS, CONV_DIM), BF16),
            pltpu.VMEM((tq, GM_WIDTH), BF16),
            pltpu.VMEM((tq, SSM_D_INNER), BF16),
        ],
        compiler_params=pltpu.CompilerParams(
            dimension_semantics=("arbitrary", "arbitrary"),
            vmem_limit_bytes=VMEM_LIMIT_BYTES),
        name="mixer",
    )(proj, dt_raw, x, gate, lp["lnw"], lp["lnb"], lp["ws"], lp["bsf"], lp["convw"], lp["convb"],
      lp["dtb"], lp["alog"], lp["dskip"], lp["nrmw"], lp["wpa"], lp["wpb"], lp["wo"], e2, fnw)


def _pad_lanes(v):
    return jnp.pad(v, (0, LANES - v.shape[0])).reshape(1, LANES)


def kernel(x, c, ada_w, ada_b, norm_w, w_in, gm_ln_w, gm_ln_b, gm_ws, gm_bs, conv_w, conv_b, dt_bias, a_log,
           d_skip, ssm_norm_w, w_proj_a, w_proj_b, w_out, final_norm_w):
    depth = w_in.shape[0]
    b, s, d = x.shape
    tm = min(1024, s)
    tq = min(256, s)

    mod = _adaln_mod(c, ada_w, ada_b)
    head_of_chan = jnp.arange(SSM_D_INNER, dtype=jnp.int32) // SSM_HEAD_DIM
    e1 = (jnp.arange(LANES, dtype=jnp.int32)[:, None] == head_of_chan[None, :]).astype(BF16)
    e2 = jnp.concatenate([e1, e1], axis=0)
    fnw = final_norm_w.reshape(1, d)

    for l in range(depth):
        shift = mod[l, :, 0:d].reshape(b, 1, d)
        scale = mod[l, :, d:2 * d].reshape(b, 1, d)
        gate = mod[l, :, 2 * d:3 * d].reshape(b, 1, d)
        wl = w_in[l]
        w_main = jnp.concatenate([wl[:, :DT_COL0], wl[:, DT_COL0 + SSM_HEADS:]], axis=1).astype(BF16)
        w_dt = jnp.pad(wl[:, DT_COL0:DT_COL0 + SSM_HEADS], ((0, 0), (0, LANES - SSM_HEADS))).astype(BF16)
        proj, dt_raw = _in_proj(x, norm_w[l], scale, shift, w_main, w_dt, tm=tm, tn=1024)
        lp = dict(
            lnw=gm_ln_w[l].reshape(1, GM_WIDTH), lnb=gm_ln_b[l].reshape(1, GM_WIDTH),
            ws=gm_ws[l],
            bsf=jnp.repeat(gm_bs[l].T, CHUNK, axis=1),
            convw=conv_w[l], convb=conv_b[l].reshape(1, CONV_DIM),
            dtb=_pad_lanes(dt_bias[l]), alog=_pad_lanes(a_log[l]),
            dskip=jnp.repeat(d_skip[l], SSM_HEAD_DIM).reshape(1, SSM_D_INNER),
            nrmw=ssm_norm_w[l].reshape(1, SSM_D_INNER),
            wpa=w_proj_a[l].astype(BF16), wpb=w_proj_b[l].astype(BF16), wo=w_out[l].astype(BF16),
        )
        x = _mixer(proj, dt_raw, x, gate, lp, e2, fnw, tq=tq, final_norm=(l == depth - 1))
    return x
```

```python
import functools

import jax
import jax.numpy as jnp
from jax import lax
from jax.experimental import pallas as pl
from jax.experimental.pallas import tpu as pltpu

F32 = jnp.float32
BF16 = jnp.bfloat16

D_MODEL = 1024
GM_WIDTH = 1024
GM_GROUPS = 8
CHUNK = 128
SSM_D_INNER = 2048
SSM_HEAD_DIM = 64
SSM_HEADS = 32
SSM_GROUPS = 8
HEADS_PER_GROUP = 4
GROUP_WIDTH = SSM_D_INNER // SSM_GROUPS
SSM_STATE = 128
BC_WIDTH = SSM_GROUPS * SSM_STATE
SSM_CONV = 4
CONV_DIM = SSM_D_INNER + 2 * BC_WIDTH
EPS = 1e-6

IN_U = 0
IN_SZ = 3 * GM_WIDTH
IN_XS = IN_SZ + SSM_D_INNER
IN_B = IN_XS + SSM_D_INNER
IN_C = IN_B + BC_WIDTH
IN_DT = IN_C + BC_WIDTH
IN_GA = IN_DT + SSM_HEADS
IN_GB = IN_GA + D_MODEL
IN_END = IN_GB + D_MODEL

SSD_TILE = 2 * CHUNK
HALF_W = (CONV_DIM + SSM_D_INNER + D_MODEL) // 2
H0_XS, H0_B, H0_GB = 0, SSM_D_INNER, SSM_D_INNER + BC_WIDTH
H1_C, H1_SZ, H1_GB = 0, BC_WIDTH, BC_WIDTH + SSM_D_INNER
GB_HALF = D_MODEL // 2
PROJ_TILE = 256
PROJ_AFTER_CONV = 5
PROJ_AFTER_DT = 6
assert PROJ_AFTER_DT + SSM_GROUPS == HALF_W // PROJ_TILE

LANES = 128
TAIL_ROWS = 16

VMEM_LIMIT_BYTES = 56 * 1024 * 1024


def _sigmoid(x):
    return 0.5 * jnp.tanh(0.5 * x) + 0.5


def _silu(x):
    hx = 0.5 * x
    return hx * jnp.tanh(hx) + hx


def _gelu_tanh(x):
    c = 0.7978845608028654
    hx = 0.5 * x
    return hx * jnp.tanh(x * (c + (0.044715 * c) * (x * x))) + hx


def _split3(v):
    hi = v.astype(BF16)
    r1 = v - hi.astype(F32)
    mid = r1.astype(BF16)
    lo = (r1 - mid.astype(F32)).astype(BF16)
    return hi, mid, lo


def _split2(v):
    hi = v.astype(BF16)
    lo = (v - hi.astype(F32)).astype(BF16)
    return hi, lo


def _adaln_rmsnorm(x, nw, scale, shift):
    y = x * lax.rsqrt(jnp.mean(x * x, axis=-1, keepdims=True) + EPS)
    return ((y * nw) * (1.0 + scale) + shift).astype(BF16)


def _adaln_kernel(c_ref, w_ref, b_ref, o_ref):
    c = c_ref[...]
    s = c * _sigmoid(c)
    w = w_ref[0]
    s_hi, s_lo = _split2(s)
    w_hi, w_lo = _split2(w)
    acc = jnp.dot(s_hi, w_hi, preferred_element_type=F32)
    acc = acc + jnp.dot(s_lo, w_hi, preferred_element_type=F32)
    acc = acc + jnp.dot(s_hi, w_lo, preferred_element_type=F32)
    o_ref[0] = acc + b_ref[0]


def _adaln_mod(c, ada_w, ada_b):
    depth, d, n3 = ada_w.shape
    b = c.shape[0]
    tn = 1024
    return pl.pallas_call(
        _adaln_kernel,
        out_shape=jax.ShapeDtypeStruct((depth, b, n3), F32),
        grid=(depth, n3 // tn),
        in_specs=[
            pl.BlockSpec((b, d), lambda l, n: (0, 0)),
            pl.BlockSpec((1, d, tn), lambda l, n: (l, 0, n)),
            pl.BlockSpec((1, 1, tn), lambda l, n: (l, 0, n)),
        ],
        out_specs=pl.BlockSpec((1, b, tn), lambda l, n: (l, 0, n)),
        compiler_params=pltpu.CompilerParams(dimension_semantics=("arbitrary", "arbitrary")),
        name="adaln_mod",
    )(c, ada_w, ada_b.reshape(depth, 1, n3))


def _resident(shape):
    return pl.BlockSpec(shape, lambda *_: (0,) * len(shape), pipeline_mode=pl.Buffered(1))


def _ssd_kernel(x_ref, nw_ref, scale_ref, shift_ref, w_ref, wdt_ref, convw_ref, convb_ref, dtb_ref, alog_ref,
                dskip_ref, nrmw_ref, wpb_ref, e2_ref, o_ref,
                h_ref, pa_ref, pb_ref, dta_ref, dtb2_ref, state_ref, tail_ref, yb_ref, *, tiles_per_seq):
    t = pl.program_id(0)

    @pl.when(t == 0)
    def _():
        pb_ref[...] = jnp.zeros_like(pb_ref)
        dtb2_ref[...] = jnp.zeros_like(dtb2_ref)

    @pl.when(jnp.logical_or(t == 0, lax.rem(t - 1, tiles_per_seq) == 0))
    def _():
        state_ref[...] = jnp.zeros_like(state_ref)
        tail_ref[...] = jnp.zeros_like(tail_ref)

    step = functools.partial(
        _ssd_step, x_ref, nw_ref, scale_ref, shift_ref, w_ref, wdt_ref, convw_ref, convb_ref, dtb_ref, alog_ref,
        dskip_ref, nrmw_ref, wpb_ref, e2_ref, o_ref, h_ref, state_ref, tail_ref, yb_ref)

    @pl.when(lax.rem(t, 2) == 0)
    def _():
        step(pa_ref, dta_ref, pb_ref, dtb2_ref)

    @pl.when(lax.rem(t, 2) == 1)
    def _():
        step(pb_ref, dtb2_ref, pa_ref, dta_ref)


def _ssd_step(x_ref, nw_ref, scale_ref, shift_ref, w_ref, wdt_ref, convw_ref, convb_ref, dtb_ref, alog_ref,
              dskip_ref, nrmw_ref, wpb_ref, e2_ref, o_ref, h_ref, state_ref, tail_ref, yb_ref,
              pw_ref, dtw_ref, pr_ref, dtr_ref):
    hb = _adaln_rmsnorm(x_ref[0], nw_ref[...], scale_ref[0], shift_ref[0])
    h_ref[...] = hb
    dtw_ref[...] = jnp.dot(hb, wdt_ref[...], preferred_element_type=F32)

    row = lax.broadcasted_iota(jnp.int32, (CHUNK, CHUNK), 0)
    col = lax.broadcasted_iota(jnp.int32, (CHUNK, CHUNK), 1)
    tril = row >= col
    tri_ones = jnp.where(tril, 1.0, 0.0).astype(BF16)
    srow = lax.broadcasted_iota(jnp.int32, (CHUNK, TAIL_ROWS + CHUNK), 0)
    scol = lax.broadcasted_iota(jnp.int32, (CHUNK, TAIL_ROWS + CHUNK), 1)
    shifts = [jnp.where(scol == srow + (TAIL_ROWS - j), 1.0, 0.0).astype(BF16) for j in range(1, SSM_CONV)]
    lane_blk = lax.broadcasted_iota(jnp.int32, (CHUNK, GROUP_WIDTH), 1) // SSM_HEAD_DIM
    a_neg = -jnp.exp(alog_ref[...])
    convw = convw_ref[...]

    def chunk_body(c, carry):
        def project(k0, k1):
            cs = slice(k0 * PROJ_TILE, k1 * PROJ_TILE)
            pw_ref[c, :, cs] = jnp.dot(h_ref[...], w_ref[c, :, cs], preferred_element_type=F32).astype(BF16)

        r0 = pl.multiple_of(c * CHUNK, CHUNK)
        rows = pl.ds(r0, CHUNK)

        xraw = jnp.concatenate([pr_ref[0, rows, H0_XS:H0_GB],
                                pr_ref[1, rows, H1_C:H1_SZ]], axis=1)
        xe = jnp.concatenate([tail_ref[...], xraw], axis=0)
        tail_ref[...] = xraw[CHUNK - TAIL_ROWS:, :]
        acc = xraw.astype(F32) * convw[SSM_CONV - 1:SSM_CONV, :] + convb_ref[...]
        for j in range(1, SSM_CONV):
            sh = jnp.dot(shifts[j - 1], xe, preferred_element_type=F32)
            acc = acc + sh * convw[SSM_CONV - 1 - j:SSM_CONV - j, :]
        project(0, PROJ_AFTER_CONV)
        xbc = _silu(acc)
        xs = xbc[:, :SSM_D_INNER]
        xs_b = xs.astype(BF16)
        bm_b = xbc[:, SSM_D_INNER:SSM_D_INNER + BC_WIDTH].astype(BF16)
        cm_b = xbc[:, SSM_D_INNER + BC_WIDTH:].astype(BF16)

        dtv = dtr_ref[rows, :] + dtb_ref[...]
        dt = jnp.maximum(dtv, 0.0) + jnp.log(1.0 + jnp.exp(-jnp.abs(dtv)))
        adt = dt * a_neg
        hi, mid, lo = _split3(adt)
        cs3 = jnp.dot(tri_ones, jnp.concatenate([hi, mid, lo], axis=1), preferred_element_type=F32)
        a_cs = cs3[:, :LANES] + cs3[:, LANES:2 * LANES] + cs3[:, 2 * LANES:]
        a_last = a_cs[CHUNK - 1:CHUNK, :]
        a_cs_t = a_cs.T
        dt_t = dt.T
        ea = jnp.exp(a_cs)
        w_end = dt * jnp.exp(a_last - a_cs)
        ea_hi, ea_lo = _split2(ea)
        we_hi, we_lo = _split2(w_end)
        ea_x = jnp.dot(jnp.concatenate([ea_hi, ea_lo], axis=1), e2_ref[...], preferred_element_type=F32)
        we_x = jnp.dot(jnp.concatenate([we_hi, we_lo], axis=1), e2_ref[...], preferred_element_type=F32)
        cd_x = ea_x[CHUNK - 1:CHUNK, :]
        project(PROJ_AFTER_CONV, PROJ_AFTER_DT)

        for g in range(SSM_GROUPS):
            gs = slice(g * GROUP_WIDTH, (g + 1) * GROUP_WIDTH)
            ns = slice(g * SSM_STATE, (g + 1) * SSM_STATE)
            bm_g = bm_b[:, ns]
            cm_g = cm_b[:, ns]
            xs_g = xs[:, gs]
            xs_gb = xs_b[:, gs]
            cb = lax.dot_general(cm_g, bm_g, (((1,), (1,)), ((), ())), preferred_element_type=F32)
            ms = []
            for r in range(HEADS_PER_GROUP):
                h = g * HEADS_PER_GROUP + r
                diff = a_cs[:, h:h + 1] - a_cs_t[h:h + 1, :]
                decay = jnp.where(tril, jnp.exp(diff), 0.0)
                ms.append(((cb * decay) * dt_t[h:h + 1, :]).astype(BF16))
            lhs = jnp.concatenate(ms, axis=1)
            rhs = jnp.concatenate(
                [jnp.where(lane_blk == r, xs_gb, jnp.zeros_like(xs_gb)) for r in range(HEADS_PER_GROUP)],
                axis=0)
            y_diag = jnp.dot(lhs, rhs, preferred_element_type=F32)
            st = state_ref[:, gs]
            y_off = jnp.dot(cm_g, st.astype(BF16), preferred_element_type=F32) * ea_x[:, gs]
            xw = (xs_g * we_x[:, gs]).astype(BF16)
            new_st = lax.dot_general(bm_g, xw, (((0,), (0,)), ((), ())), preferred_element_type=F32)
            state_ref[:, gs] = st * cd_x[:, gs] + new_st
            project(PROJ_AFTER_DT + g, PROJ_AFTER_DT + g + 1)
            yb = y_diag + y_off + xs_g * dskip_ref[:, gs]
            sz = pr_ref[1, rows, H1_SZ + g * GROUP_WIDTH:H1_SZ + (g + 1) * GROUP_WIDTH].astype(F32)
            yz = yb * _silu(sz)
            inv = lax.rsqrt(jnp.mean(yz * yz, axis=-1, keepdims=True) + EPS)
            yb_ref[rows, gs] = ((yz * inv) * nrmw_ref[:, gs]).astype(BF16)
        return carry

    lax.fori_loop(0, SSD_TILE // CHUNK, chunk_body, 0)

    pb = jnp.dot(yb_ref[...], wpb_ref[...], preferred_element_type=F32)
    gb = jnp.concatenate([pr_ref[0, :, H0_GB:H0_GB + GB_HALF],
                          pr_ref[1, :, H1_GB:H1_GB + GB_HALF]], axis=1).astype(F32)
    o_ref[0] = (_sigmoid(gb) * pb).astype(BF16)


def _ssd_branch(x, norm_w, scale, shift, w_halves, w_dt, lp, e2):
    b, s, d = x.shape
    tq = SSD_TILE
    assert s % tq == 0
    nt = s // tq
    last = b * nt - 1

    def in_tile(t):
        tt = jnp.minimum(t, last)
        return tt // nt, tt % nt

    def out_tile(t):
        tt = jnp.maximum(t - 1, 0)
        return tt // nt, tt % nt

    in_specs = [
        pl.BlockSpec((1, tq, d), lambda t: (*in_tile(t), 0)),
        _resident((1, d)),
        pl.BlockSpec((1, 1, d), lambda t: (in_tile(t)[0], 0, 0)),
        pl.BlockSpec((1, 1, d), lambda t: (in_tile(t)[0], 0, 0)),
        _resident((2, d, HALF_W)), _resident((d, LANES)),
        _resident((SSM_CONV, CONV_DIM)), _resident((1, CONV_DIM)),
        _resident((1, LANES)), _resident((1, LANES)),
        _resident((1, SSM_D_INNER)), _resident((1, SSM_D_INNER)),
        _resident((SSM_D_INNER, d)), _resident((2 * LANES, SSM_D_INNER)),
    ]
    return pl.pallas_call(
        functools.partial(_ssd_kernel, tiles_per_seq=nt),
        out_shape=jax.ShapeDtypeStruct((b, s, d), BF16),
        grid=(b * nt + 1,),
        in_specs=in_specs,
        out_specs=pl.BlockSpec((1, tq, d), lambda t: (*out_tile(t), 0)),
        scratch_shapes=[
            pltpu.VMEM((tq, d), BF16),
            pltpu.VMEM((2, tq, HALF_W), BF16),
            pltpu.VMEM((2, tq, HALF_W), BF16),
            pltpu.VMEM((tq, LANES), F32),
            pltpu.VMEM((tq, LANES), F32),
            pltpu.VMEM((SSM_STATE, SSM_D_INNER), F32),
            pltpu.VMEM((TAIL_ROWS, CONV_DIM), BF16),
            pltpu.VMEM((tq, SSM_D_INNER), BF16),
        ],
        compiler_params=pltpu.CompilerParams(
            dimension_semantics=("arbitrary",),
            vmem_limit_bytes=VMEM_LIMIT_BYTES),
        name="ssd_branch",
    )(x, norm_w.reshape(1, d), scale, shift, w_halves, w_dt, lp["convw"], lp["convb"], lp["dtb"], lp["alog"],
      lp["dskip"], lp["nrmw"], lp["wpb"], e2)


def _gmlp_merge_kernel(x_ref, nw_ref, scale_ref, shift_ref, gate_ref, w_ref, lnw_ref, lnb_ref, ws_ref, bsf_ref,
                       wpa_ref, wo_ref, pbg_ref, fnw_ref, o_ref, *, tq, final_norm):
    x = x_ref[0]
    hb = _adaln_rmsnorm(x, nw_ref[...], scale_ref[0], shift_ref[0])
    p = jnp.dot(hb, w_ref[...], preferred_element_type=F32)
    u = _gelu_tanh(p[:, 0:GM_WIDTH])
    v = _gelu_tanh(p[:, GM_WIDTH:2 * GM_WIDTH])
    mu = jnp.mean(v, axis=-1, keepdims=True)
    vc = v - mu
    var = jnp.mean(vc * vc, axis=-1, keepdims=True)
    vn = ((vc * lax.rsqrt(var + EPS)) * lnw_ref[...] + lnb_ref[...]).astype(BF16)

    row = lax.broadcasted_iota(jnp.int32, (CHUNK, CHUNK), 0)
    col = lax.broadcasted_iota(jnp.int32, (CHUNK, CHUNK), 1)
    tril = row >= col
    ws = [jnp.where(tril, ws_ref[g], 0.0).astype(BF16) for g in range(GM_GROUPS)]
    mixed = jnp.concatenate([
        jnp.concatenate(
            [jnp.dot(ws[g], vn[k * CHUNK:(k + 1) * CHUNK, g * CHUNK:(g + 1) * CHUNK], preferred_element_type=F32)
             for g in range(GM_GROUPS)], axis=1) + bsf_ref[...]
        for k in range(tq // CHUNK)], axis=0)
    ya = ((u * mixed) * _silu(p[:, 2 * GM_WIDTH:3 * GM_WIDTH])).astype(BF16)
    pa = jnp.dot(ya, wpa_ref[...], preferred_element_type=F32)
    merged = (_sigmoid(p[:, 3 * GM_WIDTH:]) * pa + pbg_ref[0].astype(F32)).astype(BF16)
    out = x + gate_ref[0] * jnp.dot(merged, wo_ref[...], preferred_element_type=F32)
    if final_norm:
        out = (out * lax.rsqrt(jnp.mean(out * out, axis=-1, keepdims=True) + EPS)) * fnw_ref[...]
    o_ref[0] = out


def _gmlp_merge(x, norm_w, scale, shift, gate, w_y, lp, pbg, fnw, *, tq, final_norm):
    b, s, d = x.shape
    tile = lambda i, j: (i, j, 0)
    per_batch = lambda i, j: (i, 0, 0)
    in_specs = [
        pl.BlockSpec((1, tq, d), tile),
        _resident((1, d)),
        pl.BlockSpec((1, 1, d), per_batch), pl.BlockSpec((1, 1, d), per_batch), pl.BlockSpec((1, 1, d), per_batch),
        _resident((d, 4 * GM_WIDTH)),
        _resident((1, GM_WIDTH)), _resident((1, GM_WIDTH)),
        _resident((GM_GROUPS, CHUNK, CHUNK)), _resident((CHUNK, GM_WIDTH)),
        _resident((GM_WIDTH, d)), _resident((d, d)),
        pl.BlockSpec((1, tq, d), tile),
        _resident((1, d)),
    ]
    return pl.pallas_call(
        functools.partial(_gmlp_merge_kernel, tq=tq, final_norm=final_norm),
        out_shape=jax.ShapeDtypeStruct((b, s, d), F32),
        grid=(b, s // tq),
        in_specs=in_specs,
        out_specs=pl.BlockSpec((1, tq, d), tile),
        compiler_params=pltpu.CompilerParams(
            dimension_semantics=("arbitrary", "arbitrary"),
            vmem_limit_bytes=VMEM_LIMIT_BYTES),
        name="gmlp_merge",
    )(x, norm_w.reshape(1, d), scale, shift, gate, w_y, lp["lnw"], lp["lnb"], lp["ws"], lp["bsf"],
      lp["wpa"], lp["wo"], pbg, fnw)


def _pad_lanes(v):
    return jnp.pad(v, (0, LANES - v.shape[0])).reshape(1, LANES)


def kernel(x, c, ada_w, ada_b, norm_w, w_in, gm_ln_w, gm_ln_b, gm_ws, gm_bs, conv_w, conv_b, dt_bias, a_log,
           d_skip, ssm_norm_w, w_proj_a, w_proj_b, w_out, final_norm_w):
    depth = w_in.shape[0]
    b, s, d = x.shape
    assert w_in.shape[2] == IN_END

    mod = _adaln_mod(c, ada_w, ada_b)
    head_of_chan = jnp.arange(SSM_D_INNER, dtype=jnp.int32) // SSM_HEAD_DIM
    e1 = (jnp.arange(LANES, dtype=jnp.int32)[:, None] == head_of_chan[None, :]).astype(BF16)
    e2 = jnp.concatenate([e1, e1], axis=0)
    fnw = final_norm_w.reshape(1, d)

    for l in range(depth):
        shift = mod[l, :, 0:d].reshape(b, 1, d)
        scale = mod[l, :, d:2 * d].reshape(b, 1, d)
        gate = mod[l, :, 2 * d:3 * d].reshape(b, 1, d)
        wl = w_in[l]
        w_half0 = jnp.concatenate([wl[:, IN_XS:IN_C], wl[:, IN_GB:IN_GB + GB_HALF]], axis=1)
        w_half1 = jnp.concatenate([wl[:, IN_C:IN_DT], wl[:, IN_SZ:IN_XS], wl[:, IN_GB + GB_HALF:IN_END]], axis=1)
        w_halves = jnp.stack([w_half0, w_half1]).astype(BF16)
        w_dt = jnp.pad(wl[:, IN_DT:IN_GA], ((0, 0), (0, LANES - SSM_HEADS))).astype(BF16)
        w_y = jnp.concatenate([wl[:, IN_U:IN_SZ], wl[:, IN_GA:IN_GB]], axis=1).astype(BF16)
        lp = dict(
            lnw=gm_ln_w[l].reshape(1, GM_WIDTH), lnb=gm_ln_b[l].reshape(1, GM_WIDTH),
            ws=gm_ws[l],
            bsf=jnp.repeat(gm_bs[l].T, CHUNK, axis=1),
            convw=conv_w[l], convb=conv_b[l].reshape(1, CONV_DIM),
            dtb=_pad_lanes(dt_bias[l]), alog=_pad_lanes(a_log[l]),
            dskip=jnp.repeat(d_skip[l], SSM_HEAD_DIM).reshape(1, SSM_D_INNER),
            nrmw=ssm_norm_w[l].reshape(1, SSM_D_INNER),
            wpa=w_proj_a[l].astype(BF16), wpb=w_proj_b[l].astype(BF16), wo=w_out[l].astype(BF16),
        )
        pbg = _ssd_branch(x, norm_w[l], scale, shift, w_halves, w_dt, lp, e2)
        x = _gmlp_merge(x, norm_w[l], scale, shift, gate, w_y, lp, pbg, fnw,
                        tq=min(256, s), final_norm=(l == depth - 1))
    return x
```

```python
import functools

import jax
import jax.numpy as jnp
from jax import lax
from jax.experimental import pallas as pl
from jax.experimental.pallas import tpu as pltpu

F32 = jnp.float32
BF16 = jnp.bfloat16

D_MODEL = 1024
GM_WIDTH = 1024
GM_GROUPS = 8
CHUNK = 128
SSM_D_INNER = 2048
SSM_HEAD_DIM = 64
SSM_HEADS = 32
SSM_GROUPS = 8
HEADS_PER_GROUP = 4
GROUP_WIDTH = SSM_D_INNER // SSM_GROUPS
SSM_STATE = 128
BC_WIDTH = SSM_GROUPS * SSM_STATE
SSM_CONV = 4
CONV_DIM = SSM_D_INNER + 2 * BC_WIDTH
EPS = 1e-6

IN_U = 0
IN_SZ = 3 * GM_WIDTH
IN_XS = IN_SZ + SSM_D_INNER
IN_B = IN_XS + SSM_D_INNER
IN_C = IN_B + BC_WIDTH
IN_DT = IN_C + BC_WIDTH
IN_GA = IN_DT + SSM_HEADS
IN_GB = IN_GA + D_MODEL
IN_END = IN_GB + D_MODEL

SSD_TILE = 2 * CHUNK
HALF_W = (CONV_DIM + SSM_D_INNER + D_MODEL) // 2
H0_GB = SSM_D_INNER + BC_WIDTH
H1_SZ, H1_GB = BC_WIDTH, BC_WIDTH + SSM_D_INNER
GB_HALF = D_MODEL // 2
PROJ_TILE = 256
PROJ_TILES = HALF_W // PROJ_TILE
STRIP = 256
N_STRIPS = CONV_DIM // STRIP
STRIPS_PER_PROJ = 3
PROJ_AFTER_CONV = N_STRIPS // STRIPS_PER_PROJ
PROJ_AFTER_DT = PROJ_AFTER_CONV + 1
assert PROJ_AFTER_DT + SSM_GROUPS == PROJ_TILES

LANES = 128
TAIL_ROWS = 16
XE_ROWS = TAIL_ROWS + CHUNK

VMEM_LIMIT_BYTES = 56 * 1024 * 1024


def _sigmoid(x):
    return 0.5 * jnp.tanh(0.5 * x) + 0.5


def _silu(x):
    hx = 0.5 * x
    return hx * jnp.tanh(hx) + hx


def _gelu_tanh(x):
    c = 0.7978845608028654
    hx = 0.5 * x
    return hx * jnp.tanh(x * (c + (0.044715 * c) * (x * x))) + hx


def _split3(v):
    hi = v.astype(BF16)
    r1 = v - hi.astype(F32)
    mid = r1.astype(BF16)
    lo = (r1 - mid.astype(F32)).astype(BF16)
    return hi, mid, lo


def _split2(v):
    hi = v.astype(BF16)
    lo = (v - hi.astype(F32)).astype(BF16)
    return hi, lo


def _adaln_rmsnorm(x, nw, scale, shift):
    y = x * lax.rsqrt(jnp.mean(x * x, axis=-1, keepdims=True) + EPS)
    return ((y * nw) * (1.0 + scale) + shift).astype(BF16)


def _adaln_kernel(c_ref, w_ref, b_ref, o_ref):
    c = c_ref[...]
    s = c * _sigmoid(c)
    w = w_ref[0]
    s_hi, s_lo = _split2(s)
    w_hi, w_lo = _split2(w)
    acc = jnp.dot(s_hi, w_hi, preferred_element_type=F32)
    acc = acc + jnp.dot(s_lo, w_hi, preferred_element_type=F32)
    acc = acc + jnp.dot(s_hi, w_lo, preferred_element_type=F32)
    o_ref[0] = acc + b_ref[0]


def _adaln_mod(c, ada_w, ada_b):
    depth, d, n3 = ada_w.shape
    b = c.shape[0]
    tn = 1024
    return pl.pallas_call(
        _adaln_kernel,
        out_shape=jax.ShapeDtypeStruct((depth, b, n3), F32),
        grid=(depth, n3 // tn),
        in_specs=[
            pl.BlockSpec((b, d), lambda l, n: (0, 0)),
            pl.BlockSpec((1, d, tn), lambda l, n: (l, 0, n)),
            pl.BlockSpec((1, 1, tn), lambda l, n: (l, 0, n)),
        ],
        out_specs=pl.BlockSpec((1, b, tn), lambda l, n: (l, 0, n)),
        compiler_params=pltpu.CompilerParams(dimension_semantics=("arbitrary", "arbitrary")),
        name="adaln_mod",
    )(c, ada_w, ada_b.reshape(depth, 1, n3))


def _resident(shape, layer=None):
    if layer is None:
        return pl.BlockSpec(shape, lambda *_: (0,) * len(shape), pipeline_mode=pl.Buffered(1))
    return pl.BlockSpec((None,) + shape, lambda *_: (layer,) + (0,) * len(shape), pipeline_mode=pl.Buffered(1))


MOD_SHIFT, MOD_SCALE, MOD_GATE = 0, 1, 2


def _mod_spec(layer, which, d, batch_of):
    return pl.BlockSpec((None, None, None, 1, d), lambda *idx: (layer, batch_of(*idx), which, 0, 0))


def _ssd_kernel(x_ref, nw_ref, scale_ref, shift_ref, w_ref, wdt_ref, convw_ref, convb_ref, dtb_ref, alog_ref,
                dskip_ref, nrmw_ref, wpb_ref, e2_ref, o_ref,
                h_ref, pa_ref, pb_ref, dta_ref, dtb2_ref, state_ref, tail_ref, yb_ref, *, tiles_per_seq):
    t = pl.program_id(0)

    @pl.when(t == 0)
    def _():
        pb_ref[...] = jnp.zeros_like(pb_ref)
        dtb2_ref[...] = jnp.zeros_like(dtb2_ref)

    @pl.when(jnp.logical_or(t == 0, lax.rem(t - 1, tiles_per_seq) == 0))
    def _():
        state_ref[...] = jnp.zeros_like(state_ref)
        tail_ref[...] = jnp.zeros_like(tail_ref)

    step = functools.partial(
        _ssd_step, x_ref, nw_ref, scale_ref, shift_ref, w_ref, wdt_ref, convw_ref, convb_ref, dtb_ref, alog_ref,
        dskip_ref, nrmw_ref, wpb_ref, e2_ref, o_ref, h_ref, state_ref, tail_ref, yb_ref)

    @pl.when(lax.rem(t, 2) == 0)
    def _():
        step(pa_ref, dta_ref, pb_ref, dtb2_ref)

    @pl.when(lax.rem(t, 2) == 1)
    def _():
        step(pb_ref, dtb2_ref, pa_ref, dta_ref)


def _ssd_step(x_ref, nw_ref, scale_ref, shift_ref, w_ref, wdt_ref, convw_ref, convb_ref, dtb_ref, alog_ref,
              dskip_ref, nrmw_ref, wpb_ref, e2_ref, o_ref, h_ref, state_ref, tail_ref, yb_ref,
              pw_ref, dtw_ref, pr_ref, dtr_ref):
    hb = _adaln_rmsnorm(x_ref[0], nw_ref[...], scale_ref[...], shift_ref[...])
    h_ref[...] = hb
    dtw_ref[...] = jnp.dot(hb, wdt_ref[...], preferred_element_type=F32)

    row = lax.broadcasted_iota(jnp.int32, (CHUNK, CHUNK), 0)
    col = lax.broadcasted_iota(jnp.int32, (CHUNK, CHUNK), 1)
    tril = row >= col
    tri_ones = jnp.where(tril, 1.0, 0.0).astype(BF16)
    srow = lax.broadcasted_iota(jnp.int32, (CHUNK, (SSM_CONV - 1) * XE_ROWS), 0)
    scol = lax.broadcasted_iota(jnp.int32, (CHUNK, (SSM_CONV - 1) * XE_ROWS), 1)
    pick = functools.reduce(
        jnp.logical_or,
        [scol == (j - 1) * XE_ROWS + srow + (TAIL_ROWS - j) for j in range(1, SSM_CONV)])
    shift_sum = jnp.where(pick, 1.0, 0.0).astype(BF16)
    lane_blk = lax.broadcasted_iota(jnp.int32, (CHUNK, GROUP_WIDTH), 1) // SSM_HEAD_DIM
    a_neg = -jnp.exp(alog_ref[...])

    def chunk_body(c, carry):
        def project(k0, k1):
            cs = slice(k0 * PROJ_TILE, k1 * PROJ_TILE)
            pw_ref[c, :, cs] = jnp.dot(h_ref[...], w_ref[c, :, cs], preferred_element_type=F32).astype(BF16)

        r0 = pl.multiple_of(c * CHUNK, CHUNK)
        rows = pl.ds(r0, CHUNK)

        acts = []
        for s in range(N_STRIPS):
            col0 = s * STRIP
            cols = slice(col0, col0 + STRIP)
            half, off = (0, col0) if col0 < H0_GB else (1, col0 - H0_GB)
            x_s = pr_ref[half, rows, off:off + STRIP]
            xe = jnp.concatenate([tail_ref[:, cols], x_s], axis=0)
            tail_ref[:, cols] = x_s[CHUNK - TAIL_ROWS:, :]
            wb = convw_ref[:, cols].astype(BF16)
            delayed = jnp.concatenate(
                [xe * wb[SSM_CONV - 1 - j:SSM_CONV - j, :] for j in range(1, SSM_CONV)], axis=0)
            acc = jnp.dot(shift_sum, delayed, preferred_element_type=F32)
            acc = acc + (x_s.astype(F32) * convw_ref[SSM_CONV - 1:SSM_CONV, cols] + convb_ref[:, cols])
            acts.append(_silu(acc))
            if s % STRIPS_PER_PROJ == STRIPS_PER_PROJ - 1:
                project(s // STRIPS_PER_PROJ, s // STRIPS_PER_PROJ + 1)
        xs_strips = acts[:SSM_D_INNER // STRIP]
        bm_strips = acts[SSM_D_INNER // STRIP:(SSM_D_INNER + BC_WIDTH) // STRIP]
        cm_strips = acts[(SSM_D_INNER + BC_WIDTH) // STRIP:]

        dtv = dtr_ref[rows, :] + dtb_ref[...]
        dt = jnp.maximum(dtv, 0.0) + jnp.log(1.0 + jnp.exp(-jnp.abs(dtv)))
        adt = dt * a_neg
        hi, mid, lo = _split3(adt)
        cs3 = jnp.dot(tri_ones, jnp.concatenate([hi, mid, lo], axis=1), preferred_element_type=F32)
        a_cs = cs3[:, :LANES] + cs3[:, LANES:2 * LANES] + cs3[:, 2 * LANES:]
        a_last = a_cs[CHUNK - 1:CHUNK, :]
        a_cs_t = a_cs.T
        dt_t = dt.T
        ea = jnp.exp(a_cs)
        w_end = dt * jnp.exp(a_last - a_cs)
        ea_hi, ea_lo = _split2(ea)
        we_hi, we_lo = _split2(w_end)
        ea_cat = jnp.concatenate([ea_hi, ea_lo], axis=1)
        we_cat = jnp.concatenate([we_hi, we_lo], axis=1)
        project(PROJ_AFTER_CONV, PROJ_AFTER_DT)

        for g in range(SSM_GROUPS):
            gs = slice(g * GROUP_WIDTH, (g + 1) * GROUP_WIDTH)
            ea_x = jnp.dot(ea_cat, e2_ref[:, gs], preferred_element_type=F32)
            we_x = jnp.dot(we_cat, e2_ref[:, gs], preferred_element_type=F32)
            cd_x = ea_x[CHUNK - 1:CHUNK, :]
            bc_half = slice((g % 2) * SSM_STATE, (g % 2 + 1) * SSM_STATE)
            bm_g = bm_strips[g // 2][:, bc_half].astype(BF16)
            cm_g = cm_strips[g // 2][:, bc_half].astype(BF16)
            xs_g = xs_strips[g]
            xs_gb = xs_g.astype(BF16)
            cb = lax.dot_general(cm_g, bm_g, (((1,), (1,)), ((), ())), preferred_element_type=F32)
            ms = []
            for r in range(HEADS_PER_GROUP):
                h = g * HEADS_PER_GROUP + r
                diff = a_cs[:, h:h + 1] - a_cs_t[h:h + 1, :]
                decay = jnp.where(tril, jnp.exp(diff), 0.0)
                ms.append(((cb * decay) * dt_t[h:h + 1, :]).astype(BF16))
            lhs = jnp.concatenate(ms, axis=1)
            rhs = jnp.concatenate(
                [jnp.where(lane_blk == r, xs_gb, jnp.zeros_like(xs_gb)) for r in range(HEADS_PER_GROUP)],
                axis=0)
            y_diag = jnp.dot(lhs, rhs, preferred_element_type=F32)
            st = state_ref[:, gs]
            y_off = jnp.dot(cm_g, st.astype(BF16), preferred_element_type=F32) * ea_x
            xw = (xs_g * we_x).astype(BF16)
            new_st = lax.dot_general(bm_g, xw, (((0,), (0,)), ((), ())), preferred_element_type=F32)
            state_ref[:, gs] = st * cd_x + new_st
            project(PROJ_AFTER_DT + g, PROJ_AFTER_DT + g + 1)
            yb = y_diag + y_off + xs_g * dskip_ref[:, gs]
            sz = pr_ref[1, rows, H1_SZ + g * GROUP_WIDTH:H1_SZ + (g + 1) * GROUP_WIDTH].astype(F32)
            yz = yb * _silu(sz)
            inv = lax.rsqrt(jnp.mean(yz * yz, axis=-1, keepdims=True) + EPS)
            yb_ref[rows, gs] = ((yz * inv) * nrmw_ref[:, gs]).astype(BF16)
        return carry

    lax.fori_loop(0, SSD_TILE // CHUNK, chunk_body, 0)

    pb = jnp.dot(yb_ref[...], wpb_ref[...], preferred_element_type=F32)
    gb = jnp.concatenate([pr_ref[0, :, H0_GB:H0_GB + GB_HALF],
                          pr_ref[1, :, H1_GB:H1_GB + GB_HALF]], axis=1).astype(F32)
    o_ref[0] = (_sigmoid(gb) * pb).astype(BF16)


def _ssd_branch(x, mod, layer, lp, e2):
    b, s, d = x.shape
    tq = SSD_TILE
    assert s % tq == 0
    nt = s // tq
    last = b * nt - 1

    def in_tile(t):
        tt = jnp.minimum(t, last)
        return tt // nt, tt % nt

    def out_tile(t):
        tt = jnp.maximum(t - 1, 0)
        return tt // nt, tt % nt

    batch_in = lambda t: in_tile(t)[0]
    in_specs = [
        pl.BlockSpec((1, tq, d), lambda t: (*in_tile(t), 0)),
        _resident((1, d), layer),
        _mod_spec(layer, MOD_SCALE, d, batch_in),
        _mod_spec(layer, MOD_SHIFT, d, batch_in),
        _resident((2, d, HALF_W), layer), _resident((d, LANES), layer),
        _resident((SSM_CONV, CONV_DIM), layer), _resident((1, CONV_DIM), layer),
        _resident((1, LANES), layer), _resident((1, LANES), layer),
        _resident((1, SSM_D_INNER), layer), _resident((1, SSM_D_INNER), layer),
        _resident((SSM_D_INNER, d), layer), _resident((2 * LANES, SSM_D_INNER)),
    ]
    return pl.pallas_call(
        functools.partial(_ssd_kernel, tiles_per_seq=nt),
        out_shape=jax.ShapeDtypeStruct((b, s, d), BF16),
        grid=(b * nt + 1,),
        in_specs=in_specs,
        out_specs=pl.BlockSpec((1, tq, d), lambda t: (*out_tile(t), 0)),
        scratch_shapes=[
            pltpu.VMEM((tq, d), BF16),
            pltpu.VMEM((2, tq, HALF_W), BF16),
            pltpu.VMEM((2, tq, HALF_W), BF16),
            pltpu.VMEM((tq, LANES), F32),
            pltpu.VMEM((tq, LANES), F32),
            pltpu.VMEM((SSM_STATE, SSM_D_INNER), F32),
            pltpu.VMEM((TAIL_ROWS, CONV_DIM), BF16),
            pltpu.VMEM((tq, SSM_D_INNER), BF16),
        ],
        compiler_params=pltpu.CompilerParams(
            dimension_semantics=("arbitrary",),
            vmem_limit_bytes=VMEM_LIMIT_BYTES),
        name="ssd_branch",
    )(x, lp["nw"], mod, mod, lp["w_halves"], lp["w_dt"], lp["convw"], lp["convb"], lp["dtb"], lp["alog"],
      lp["dskip"], lp["nrmw"], lp["wpb"], e2)


def _gmlp_merge_kernel(x_ref, nw_ref, scale_ref, shift_ref, gate_ref, w_ref, lnw_ref, lnb_ref, ws_ref, bsf_ref,
                       wpa_ref, wo_ref, pbg_ref, fnw_ref, o_ref, *, tq, final_norm):
    x = x_ref[0]
    hb = _adaln_rmsnorm(x, nw_ref[...], scale_ref[...], shift_ref[...])
    p = jnp.dot(hb, w_ref[...], preferred_element_type=F32)
    u = _gelu_tanh(p[:, 0:GM_WIDTH])
    v = _gelu_tanh(p[:, GM_WIDTH:2 * GM_WIDTH])
    mu = jnp.mean(v, axis=-1, keepdims=True)
    vc = v - mu
    var = jnp.mean(vc * vc, axis=-1, keepdims=True)
    vn = ((vc * lax.rsqrt(var + EPS)) * lnw_ref[...] + lnb_ref[...]).astype(BF16)

    row = lax.broadcasted_iota(jnp.int32, (CHUNK, CHUNK), 0)
    col = lax.broadcasted_iota(jnp.int32, (CHUNK, CHUNK), 1)
    tril = row >= col
    ws = [jnp.where(tril, ws_ref[g], 0.0).astype(BF16) for g in range(GM_GROUPS)]
    mixed = jnp.concatenate([
        jnp.concatenate(
            [jnp.dot(ws[g], vn[k * CHUNK:(k + 1) * CHUNK, g * CHUNK:(g + 1) * CHUNK], preferred_element_type=F32)
             for g in range(GM_GROUPS)], axis=1) + bsf_ref[...]
        for k in range(tq // CHUNK)], axis=0)
    ya = ((u * mixed) * _silu(p[:, 2 * GM_WIDTH:3 * GM_WIDTH])).astype(BF16)
    pa = jnp.dot(ya, wpa_ref[...], preferred_element_type=F32)
    merged = (_sigmoid(p[:, 3 * GM_WIDTH:]) * pa + pbg_ref[0].astype(F32)).astype(BF16)
    out = x + gate_ref[...] * jnp.dot(merged, wo_ref[...], preferred_element_type=F32)
    if final_norm:
        out = (out * lax.rsqrt(jnp.mean(out * out, axis=-1, keepdims=True) + EPS)) * fnw_ref[...]
    o_ref[0] = out


def _gmlp_merge(x, mod, layer, lp, pbg, fnw, *, tq, final_norm):
    b, s, d = x.shape
    tile = lambda i, j: (i, j, 0)
    batch_of = lambda i, j: i
    in_specs = [
        pl.BlockSpec((1, tq, d), tile),
        _resident((1, d), layer),
        _mod_spec(layer, MOD_SCALE, d, batch_of), _mod_spec(layer, MOD_SHIFT, d, batch_of),
        _mod_spec(layer, MOD_GATE, d, batch_of),
        _resident((d, 4 * GM_WIDTH), layer),
        _resident((1, GM_WIDTH), layer), _resident((1, GM_WIDTH), layer),
        _resident((GM_GROUPS, CHUNK, CHUNK), layer), _resident((CHUNK, GM_WIDTH), layer),
        _resident((GM_WIDTH, d), layer), _resident((d, d), layer),
        pl.BlockSpec((1, tq, d), tile),
        _resident((1, d)),
    ]
    return pl.pallas_call(
        functools.partial(_gmlp_merge_kernel, tq=tq, final_norm=final_norm),
        out_shape=jax.ShapeDtypeStruct((b, s, d), F32),
        grid=(b, s // tq),
        in_specs=in_specs,
        out_specs=pl.BlockSpec((1, tq, d), tile),
        compiler_params=pltpu.CompilerParams(
            dimension_semantics=("arbitrary", "arbitrary"),
            vmem_limit_bytes=VMEM_LIMIT_BYTES),
        name="gmlp_merge",
    )(x, lp["nw"], mod, mod, mod, lp["w_y"], lp["lnw"], lp["lnb"], lp["ws"], lp["bsf"],
      lp["wpa"], lp["wo"], pbg, fnw)


def _pad_lanes(v):
    return jnp.pad(v, ((0, 0), (0, LANES - v.shape[1])))[:, None, :]


def kernel(x, c, ada_w, ada_b, norm_w, w_in, gm_ln_w, gm_ln_b, gm_ws, gm_bs, conv_w, conv_b, dt_bias, a_log,
           d_skip, ssm_norm_w, w_proj_a, w_proj_b, w_out, final_norm_w):
    depth = w_in.shape[0]
    b, s, d = x.shape
    assert w_in.shape[2] == IN_END

    mod = _adaln_mod(c, ada_w, ada_b).reshape(depth, b, 3, 1, d)
    head_of_chan = jnp.arange(SSM_D_INNER, dtype=jnp.int32) // SSM_HEAD_DIM
    e1 = (jnp.arange(LANES, dtype=jnp.int32)[:, None] == head_of_chan[None, :]).astype(BF16)
    e2 = jnp.concatenate([e1, e1], axis=0)
    fnw = final_norm_w.reshape(1, d)

    w_half0 = jnp.concatenate([w_in[:, :, IN_XS:IN_C], w_in[:, :, IN_GB:IN_GB + GB_HALF]], axis=2)
    w_half1 = jnp.concatenate([w_in[:, :, IN_C:IN_DT], w_in[:, :, IN_SZ:IN_XS],
                               w_in[:, :, IN_GB + GB_HALF:IN_END]], axis=2)
    lp = dict(
        nw=norm_w[:, None, :],
        w_halves=jnp.stack([w_half0, w_half1], axis=1).astype(BF16),
        w_dt=jnp.pad(w_in[:, :, IN_DT:IN_GA], ((0, 0), (0, 0), (0, LANES - SSM_HEADS))).astype(BF16),
        w_y=jnp.concatenate([w_in[:, :, IN_U:IN_SZ], w_in[:, :, IN_GA:IN_GB]], axis=2).astype(BF16),
        lnw=gm_ln_w[:, None, :], lnb=gm_ln_b[:, None, :],
        ws=gm_ws,
        bsf=jnp.repeat(jnp.swapaxes(gm_bs, 1, 2), CHUNK, axis=2),
        convw=conv_w, convb=conv_b[:, None, :],
        dtb=_pad_lanes(dt_bias), alog=_pad_lanes(a_log),
        dskip=jnp.repeat(d_skip, SSM_HEAD_DIM, axis=1)[:, None, :],
        nrmw=ssm_norm_w[:, None, :],
        wpa=w_proj_a.astype(BF16), wpb=w_proj_b.astype(BF16), wo=w_out.astype(BF16),
    )
    for l in range(depth):
        pbg = _ssd_branch(x, mod, l, lp, e2)
        x = _gmlp_merge(x, mod, l, lp, pbg, fnw, tq=min(512, s), final_norm=(l == depth - 1))
    return x
```

```python
import functools

import jax
import jax.numpy as jnp
from jax import lax
from jax.experimental import pallas as pl
from jax.experimental.pallas import tpu as pltpu

F32 = jnp.float32
BF16 = jnp.bfloat16

D_MODEL = 1024
GM_WIDTH = 1024
GM_GROUPS = 8
CHUNK = 128
SSM_D_INNER = 2048
SSM_HEAD_DIM = 64
SSM_HEADS = 32
SSM_GROUPS = 8
HEADS_PER_GROUP = 4
GROUP_WIDTH = SSM_D_INNER // SSM_GROUPS
SSM_STATE = 128
BC_WIDTH = SSM_GROUPS * SSM_STATE
SSM_CONV = 4
CONV_DIM = SSM_D_INNER + 2 * BC_WIDTH
EPS = 1e-6

IN_U = 0
IN_SZ = 3 * GM_WIDTH
IN_XS = IN_SZ + SSM_D_INNER
IN_B = IN_XS + SSM_D_INNER
IN_C = IN_B + BC_WIDTH
IN_DT = IN_C + BC_WIDTH
IN_GA = IN_DT + SSM_HEADS
IN_GB = IN_GA + D_MODEL
IN_END = IN_GB + D_MODEL
W_BLOCK = 1024
assert all(off % W_BLOCK == 0 for off in (IN_SZ, IN_XS, IN_B, IN_C))

SSD_TILE = 2 * CHUNK
HALF_W = (CONV_DIM + SSM_D_INNER + D_MODEL) // 2
H0_GB = SSM_D_INNER + BC_WIDTH
H1_SZ, H1_GB = BC_WIDTH, BC_WIDTH + SSM_D_INNER
GB_HALF = D_MODEL // 2
PROJ_TILE = 256
PROJ_TILES = HALF_W // PROJ_TILE
STRIP = 256
N_STRIPS = CONV_DIM // STRIP
STRIPS_PER_PROJ = 3
PROJ_AFTER_CONV = N_STRIPS // STRIPS_PER_PROJ
PROJ_AFTER_DT = PROJ_AFTER_CONV + 1
assert PROJ_AFTER_DT + SSM_GROUPS == PROJ_TILES

LANES = 128
TAIL_ROWS = 16
XE_ROWS = TAIL_ROWS + CHUNK

VMEM_LIMIT_BYTES = 56 * 1024 * 1024


def _sigmoid(x):
    return 0.5 * jnp.tanh(0.5 * x) + 0.5


def _silu(x):
    hx = 0.5 * x
    return hx * jnp.tanh(hx) + hx


def _gelu_tanh(x):
    c = 0.7978845608028654
    hx = 0.5 * x
    return hx * jnp.tanh(x * (c + (0.044715 * c) * (x * x))) + hx


def _split3(v):
    hi = v.astype(BF16)
    r1 = v - hi.astype(F32)
    mid = r1.astype(BF16)
    lo = (r1 - mid.astype(F32)).astype(BF16)
    return hi, mid, lo


def _split2(v):
    hi = v.astype(BF16)
    lo = (v - hi.astype(F32)).astype(BF16)
    return hi, lo


def _adaln_rmsnorm(x, nw, scale, shift):
    y = x * lax.rsqrt(jnp.mean(x * x, axis=-1, keepdims=True) + EPS)
    return ((y * nw) * (1.0 + scale) + shift).astype(BF16)


def _adaln_kernel(c_ref, w_ref, b_ref, o_ref):
    c = c_ref[...]
    s = c * _sigmoid(c)
    w = w_ref[0]
    s_hi, s_lo = _split2(s)
    w_hi, w_lo = _split2(w)
    acc = jnp.dot(s_hi, w_hi, preferred_element_type=F32)
    acc = acc + jnp.dot(s_lo, w_hi, preferred_element_type=F32)
    acc = acc + jnp.dot(s_hi, w_lo, preferred_element_type=F32)
    o_ref[0] = acc + b_ref[0]


def _adaln_mod(c, ada_w, ada_b):
    depth, d, n3 = ada_w.shape
    b = c.shape[0]
    tn = 1024
    return pl.pallas_call(
        _adaln_kernel,
        out_shape=jax.ShapeDtypeStruct((depth, b, n3), F32),
        grid=(depth, n3 // tn),
        in_specs=[
            pl.BlockSpec((b, d), lambda l, n: (0, 0)),
            pl.BlockSpec((1, d, tn), lambda l, n: (l, 0, n)),
            pl.BlockSpec((1, 1, tn), lambda l, n: (l, 0, n)),
        ],
        out_specs=pl.BlockSpec((1, b, tn), lambda l, n: (l, 0, n)),
        compiler_params=pltpu.CompilerParams(dimension_semantics=("arbitrary", "arbitrary")),
        name="adaln_mod",
    )(c, ada_w, ada_b.reshape(depth, 1, n3))


def _resident(shape, layer=None):
    if layer is None:
        return pl.BlockSpec(shape, lambda *_: (0,) * len(shape), pipeline_mode=pl.Buffered(1))
    return pl.BlockSpec((None,) + shape, lambda *_: (layer,) + (0,) * len(shape), pipeline_mode=pl.Buffered(1))


def _w_block(layer, block, d, width=W_BLOCK):
    return pl.BlockSpec((None, d, width), lambda *_: (layer, 0, block), pipeline_mode=pl.Buffered(1))


MOD_SHIFT, MOD_SCALE, MOD_GATE = 0, 1, 2


def _mod_spec(layer, which, d, batch_of):
    return pl.BlockSpec((None, None, None, 1, d), lambda *idx: (layer, batch_of(*idx), which, 0, 0))


def _ssd_kernel(x_ref, nw_ref, scale_ref, shift_ref, wxs0_ref, wxs1_ref, wbm_ref, wcm_ref, wsz0_ref, wsz1_ref,
                wgb_ref, wdt_ref, convw_ref, convb_ref, dtb_ref, alog_ref, dskip_ref, nrmw_ref, wpb_ref, e2_ref, o_ref,
                h_ref, pa_ref, pb_ref, dta_ref, dtb2_ref, state_ref, tail_ref, yb_ref, *, tiles_per_seq):
    t = pl.program_id(0)

    @pl.when(t == 0)
    def _():
        pb_ref[...] = jnp.zeros_like(pb_ref)
        dtb2_ref[...] = jnp.zeros_like(dtb2_ref)

    @pl.when(jnp.logical_or(t == 0, lax.rem(t - 1, tiles_per_seq) == 0))
    def _():
        state_ref[...] = jnp.zeros_like(state_ref)
        tail_ref[...] = jnp.zeros_like(tail_ref)

    step = functools.partial(
        _ssd_step, x_ref, nw_ref, scale_ref, shift_ref,
        (wxs0_ref, wxs1_ref, wbm_ref, wcm_ref, wsz0_ref, wsz1_ref, wgb_ref), wdt_ref, convw_ref, convb_ref,
        dtb_ref, alog_ref, dskip_ref, nrmw_ref, wpb_ref, e2_ref, o_ref, h_ref, state_ref, tail_ref, yb_ref)

    @pl.when(lax.rem(t, 2) == 0)
    def _():
        step(pa_ref, dta_ref, pb_ref, dtb2_ref)

    @pl.when(lax.rem(t, 2) == 1)
    def _():
        step(pb_ref, dtb2_ref, pa_ref, dta_ref)


def _ssd_step(x_ref, nw_ref, scale_ref, shift_ref, w_refs, wdt_ref, convw_ref, convb_ref, dtb_ref, alog_ref,
              dskip_ref, nrmw_ref, wpb_ref, e2_ref, o_ref, h_ref, state_ref, tail_ref, yb_ref,
              pw_ref, dtw_ref, pr_ref, dtr_ref):
    wxs0, wxs1, wbm, wcm, wsz0, wsz1, wgb = w_refs
    per_block = W_BLOCK // PROJ_TILE
    block_tiles = lambda ref: [(ref, i) for i in range(per_block)]
    gb_tiles = GB_HALF // PROJ_TILE
    proj_sources = (
        block_tiles(wxs0) + block_tiles(wxs1) + block_tiles(wbm) + [(wgb, i) for i in range(gb_tiles)],
        block_tiles(wcm) + block_tiles(wsz0) + block_tiles(wsz1) + [(wgb, gb_tiles + i) for i in range(gb_tiles)],
    )
    assert all(len(srcs) == PROJ_TILES for srcs in proj_sources)
    hb = _adaln_rmsnorm(x_ref[0], nw_ref[...], scale_ref[...], shift_ref[...])
    h_ref[...] = hb
    dtw_ref[...] = jnp.dot(hb, wdt_ref[...], preferred_element_type=F32)

    row = lax.broadcasted_iota(jnp.int32, (CHUNK, CHUNK), 0)
    col = lax.broadcasted_iota(jnp.int32, (CHUNK, CHUNK), 1)
    tril = row >= col
    tri_ones = jnp.where(tril, 1.0, 0.0).astype(BF16)
    srow = lax.broadcasted_iota(jnp.int32, (CHUNK, (SSM_CONV - 1) * XE_ROWS), 0)
    scol = lax.broadcasted_iota(jnp.int32, (CHUNK, (SSM_CONV - 1) * XE_ROWS), 1)
    pick = functools.reduce(
        jnp.logical_or,
        [scol == (j - 1) * XE_ROWS + srow + (TAIL_ROWS - j) for j in range(1, SSM_CONV)])
    shift_sum = jnp.where(pick, 1.0, 0.0).astype(BF16)
    lane_blk = lax.broadcasted_iota(jnp.int32, (CHUNK, GROUP_WIDTH), 1) // SSM_HEAD_DIM
    a_neg = -jnp.exp(alog_ref[...])

    def chunk_body(c, carry):
        def project(k0, k1):
            for k in range(k0, k1):
                src_ref, i = proj_sources[c][k]
                pw_ref[c, :, k * PROJ_TILE:(k + 1) * PROJ_TILE] = jnp.dot(
                    h_ref[...], src_ref[:, i * PROJ_TILE:(i + 1) * PROJ_TILE],
                    preferred_element_type=F32).astype(BF16)

        r0 = c * CHUNK
        rows = pl.ds(r0, CHUNK)

        acts = []
        for s in range(N_STRIPS):
            col0 = s * STRIP
            cols = slice(col0, col0 + STRIP)
            half, off = (0, col0) if col0 < H0_GB else (1, col0 - H0_GB)
            x_s = pr_ref[half, rows, off:off + STRIP]
            xe = jnp.concatenate([tail_ref[:, cols], x_s], axis=0)
            tail_ref[:, cols] = x_s[CHUNK - TAIL_ROWS:, :]
            wb = convw_ref[:, cols].astype(BF16)
            delayed = jnp.concatenate(
                [xe * wb[SSM_CONV - 1 - j:SSM_CONV - j, :] for j in range(1, SSM_CONV)], axis=0)
            acc = jnp.dot(shift_sum, delayed, preferred_element_type=F32)
            acc = acc + (x_s.astype(F32) * convw_ref[SSM_CONV - 1:SSM_CONV, cols] + convb_ref[:, cols])
            acts.append(_silu(acc))
            if s % STRIPS_PER_PROJ == STRIPS_PER_PROJ - 1:
                project(s // STRIPS_PER_PROJ, s // STRIPS_PER_PROJ + 1)
        xs_strips = acts[:SSM_D_INNER // STRIP]
        bm_strips = acts[SSM_D_INNER // STRIP:(SSM_D_INNER + BC_WIDTH) // STRIP]
        cm_strips = acts[(SSM_D_INNER + BC_WIDTH) // STRIP:]

        dtv = dtr_ref[rows, :] + dtb_ref[...]
        dt = jnp.maximum(dtv, 0.0) + jnp.log(1.0 + jnp.exp(-jnp.abs(dtv)))
        adt = dt * a_neg
        hi, mid, lo = _split3(adt)
        cs3 = jnp.dot(tri_ones, jnp.concatenate([hi, mid, lo], axis=1), preferred_element_type=F32)
        a_cs = cs3[:, :LANES] + cs3[:, LANES:2 * LANES] + cs3[:, 2 * LANES:]
        a_last = a_cs[CHUNK - 1:CHUNK, :]
        a_cs_t = a_cs.T
        dt_t = dt.T
        ea = jnp.exp(a_cs)
        w_end = dt * jnp.exp(a_last - a_cs)
        ea_hi, ea_lo = _split2(ea)
        we_hi, we_lo = _split2(w_end)
        ea_cat = jnp.concatenate([ea_hi, ea_lo], axis=1)
        we_cat = jnp.concatenate([we_hi, we_lo], axis=1)
        project(PROJ_AFTER_CONV, PROJ_AFTER_DT)

        for g in range(SSM_GROUPS):
            gs = slice(g * GROUP_WIDTH, (g + 1) * GROUP_WIDTH)
            ea_x = jnp.dot(ea_cat, e2_ref[:, gs], preferred_element_type=F32)
            we_x = jnp.dot(we_cat, e2_ref[:, gs], preferred_element_type=F32)
            cd_x = ea_x[CHUNK - 1:CHUNK, :]
            bc_half = slice((g % 2) * SSM_STATE, (g % 2 + 1) * SSM_STATE)
            bm_g = bm_strips[g // 2][:, bc_half].astype(BF16)
            cm_g = cm_strips[g // 2][:, bc_half].astype(BF16)
            xs_g = xs_strips[g]
            xs_gb = xs_g.astype(BF16)
            cb = lax.dot_general(cm_g, bm_g, (((1,), (1,)), ((), ())), preferred_element_type=F32)
            ms = []
            for r in range(HEADS_PER_GROUP):
                h = g * HEADS_PER_GROUP + r
                diff = a_cs[:, h:h + 1] - a_cs_t[h:h + 1, :]
                decay = jnp.where(tril, jnp.exp(diff), 0.0)
                ms.append(((cb * decay) * dt_t[h:h + 1, :]).astype(BF16))
            lhs = jnp.concatenate(ms, axis=1)
            rhs = jnp.concatenate(
                [jnp.where(lane_blk == r, xs_gb, jnp.zeros_like(xs_gb)) for r in range(HEADS_PER_GROUP)],
                axis=0)
            y_diag = jnp.dot(lhs, rhs, preferred_element_type=F32)
            st = state_ref[:, gs]
            y_off = jnp.dot(cm_g, st.astype(BF16), preferred_element_type=F32) * ea_x
            xw = (xs_g * we_x).astype(BF16)
            new_st = lax.dot_general(bm_g, xw, (((0,), (0,)), ((), ())), preferred_element_type=F32)
            state_ref[:, gs] = st * cd_x + new_st
            project(PROJ_AFTER_DT + g, PROJ_AFTER_DT + g + 1)
            yb = y_diag + y_off + xs_g * dskip_ref[:, gs]
            sz = pr_ref[1, rows, H1_SZ + g * GROUP_WIDTH:H1_SZ + (g + 1) * GROUP_WIDTH].astype(F32)
            yz = yb * _silu(sz)
            inv = lax.rsqrt(jnp.mean(yz * yz, axis=-1, keepdims=True) + EPS)
            yb_ref[rows, gs] = ((yz * inv) * nrmw_ref[:, gs]).astype(BF16)

        pb = jnp.dot(yb_ref[rows, :], wpb_ref[...], preferred_element_type=F32)
        gb = jnp.concatenate([pr_ref[0, rows, H0_GB:H0_GB + GB_HALF],
                              pr_ref[1, rows, H1_GB:H1_GB + GB_HALF]], axis=1).astype(F32)
        o_ref[0, rows, :] = (_sigmoid(gb) * pb).astype(BF16)
        return carry

    for c in range(SSD_TILE // CHUNK):
        chunk_body(c, 0)


def _ssd_branch(x, mod, layer, lp, e2):
    b, s, d = x.shape
    tq = SSD_TILE
    assert s % tq == 0
    nt = s // tq
    last = b * nt - 1

    def in_tile(t):
        tt = jnp.minimum(t, last)
        return tt // nt, tt % nt

    def out_tile(t):
        tt = jnp.maximum(t - 1, 0)
        return tt // nt, tt % nt

    batch_in = lambda t: in_tile(t)[0]
    in_specs = [
        pl.BlockSpec((1, tq, d), lambda t: (*in_tile(t), 0)),
        _resident((1, d), layer),
        _mod_spec(layer, MOD_SCALE, d, batch_in),
        _mod_spec(layer, MOD_SHIFT, d, batch_in),
        *[_w_block(layer, off // W_BLOCK, d) for off in
          (IN_XS, IN_XS + W_BLOCK, IN_B, IN_C, IN_SZ, IN_SZ + W_BLOCK)],
        _w_block(layer, 1, d),
        _resident((d, LANES), layer),
        _resident((SSM_CONV, CONV_DIM), layer), _resident((1, CONV_DIM), layer),
        _resident((1, LANES), layer), _resident((1, LANES), layer),
        _resident((1, SSM_D_INNER), layer), _resident((1, SSM_D_INNER), layer),
        _resident((SSM_D_INNER, d), layer), _resident((2 * LANES, SSM_D_INNER)),
    ]
    return pl.pallas_call(
        functools.partial(_ssd_kernel, tiles_per_seq=nt),
        out_shape=jax.ShapeDtypeStruct((b, s, d), BF16),
        grid=(b * nt + 1,),
        in_specs=in_specs,
        out_specs=pl.BlockSpec((1, tq, d), lambda t: (*out_tile(t), 0)),
        scratch_shapes=[
            pltpu.VMEM((tq, d), BF16),
            pltpu.VMEM((2, tq, HALF_W), BF16),
            pltpu.VMEM((2, tq, HALF_W), BF16),
            pltpu.VMEM((tq, LANES), F32),
            pltpu.VMEM((tq, LANES), F32),
            pltpu.VMEM((SSM_STATE, SSM_D_INNER), F32),
            pltpu.VMEM((TAIL_ROWS, CONV_DIM), BF16),
            pltpu.VMEM((tq, SSM_D_INNER), BF16),
        ],
        compiler_params=pltpu.CompilerParams(
            dimension_semantics=("arbitrary",),
            vmem_limit_bytes=VMEM_LIMIT_BYTES),
        name="ssd_branch",
    )(x, lp["nw"], mod, mod, *([lp["w_bf"]] * 6), lp["w_gates"], lp["w_dt"], lp["convw"], lp["convb"], lp["dtb"], lp["alog"],
      lp["dskip"], lp["nrmw"], lp["wpb"], e2)


def _gmlp_merge_kernel(x_ref, nw_ref, scale_ref, shift_ref, gate_ref, wuvz_ref, wga_ref, lnw_ref, lnb_ref, ws_ref,
                       bsf_ref, wpa_ref, wo_ref, pbg_ref, fnw_ref, o_ref, *, tq, final_norm):
    x = x_ref[0]
    hb = _adaln_rmsnorm(x, nw_ref[...], scale_ref[...], shift_ref[...])
    p = jnp.dot(hb, wuvz_ref[...], preferred_element_type=F32)
    ga = jnp.dot(hb, wga_ref[...], preferred_element_type=F32)
    u = _gelu_tanh(p[:, 0:GM_WIDTH])
    v = _gelu_tanh(p[:, GM_WIDTH:2 * GM_WIDTH])
    mu = jnp.mean(v, axis=-1, keepdims=True)
    vc = v - mu
    var = jnp.mean(vc * vc, axis=-1, keepdims=True)
    vn = ((vc * lax.rsqrt(var + EPS)) * lnw_ref[...] + lnb_ref[...]).astype(BF16)

    row = lax.broadcasted_iota(jnp.int32, (CHUNK, CHUNK), 0)
    col = lax.broadcasted_iota(jnp.int32, (CHUNK, CHUNK), 1)
    tril = row >= col
    ws = [jnp.where(tril, ws_ref[g], 0.0).astype(BF16) for g in range(GM_GROUPS)]
    mixed = jnp.concatenate([
        jnp.concatenate(
            [jnp.dot(ws[g], vn[k * CHUNK:(k + 1) * CHUNK, g * CHUNK:(g + 1) * CHUNK], preferred_element_type=F32)
             for g in range(GM_GROUPS)], axis=1) + bsf_ref[...]
        for k in range(tq // CHUNK)], axis=0)
    ya = ((u * mixed) * _silu(p[:, 2 * GM_WIDTH:3 * GM_WIDTH])).astype(BF16)
    pa = jnp.dot(ya, wpa_ref[...], preferred_element_type=F32)
    merged = (_sigmoid(ga) * pa + pbg_ref[0].astype(F32)).astype(BF16)
    out = x + gate_ref[...] * jnp.dot(merged, wo_ref[...], preferred_element_type=F32)
    if final_norm:
        out = (out * lax.rsqrt(jnp.mean(out * out, axis=-1, keepdims=True) + EPS)) * fnw_ref[...]
    o_ref[0] = out


def _gmlp_merge(x, mod, layer, lp, pbg, fnw, *, tq, final_norm):
    b, s, d = x.shape
    tile = lambda i, j: (i, j, 0)
    batch_of = lambda i, j: i
    in_specs = [
        pl.BlockSpec((1, tq, d), tile),
        _resident((1, d), layer),
        _mod_spec(layer, MOD_SCALE, d, batch_of), _mod_spec(layer, MOD_SHIFT, d, batch_of),
        _mod_spec(layer, MOD_GATE, d, batch_of),
        _w_block(layer, 0, d, width=IN_SZ),
        _w_block(layer, 0, d),
        _resident((1, GM_WIDTH), layer), _resident((1, GM_WIDTH), layer),
        _resident((GM_GROUPS, CHUNK, CHUNK), layer), _resident((CHUNK, GM_WIDTH), layer),
        _resident((GM_WIDTH, d), layer), _resident((d, d), layer),
        pl.BlockSpec((1, tq, d), tile),
        _resident((1, d)),
    ]
    return pl.pallas_call(
        functools.partial(_gmlp_merge_kernel, tq=tq, final_norm=final_norm),
        out_shape=jax.ShapeDtypeStruct((b, s, d), F32),
        grid=(b, s // tq),
        in_specs=in_specs,
        out_specs=pl.BlockSpec((1, tq, d), tile),
        compiler_params=pltpu.CompilerParams(
            dimension_semantics=("arbitrary", "arbitrary"),
            vmem_limit_bytes=VMEM_LIMIT_BYTES),
        name="gmlp_merge",
    )(x, lp["nw"], mod, mod, mod, lp["w_bf"], lp["w_gates"], lp["lnw"], lp["lnb"], lp["ws"], lp["bsf"],
      lp["wpa"], lp["wo"], pbg, fnw)


def _pad_lanes(v):
    return jnp.pad(v, ((0, 0), (0, LANES - v.shape[1])))[:, None, :]


def kernel(x, c, ada_w, ada_b, norm_w, w_in, gm_ln_w, gm_ln_b, gm_ws, gm_bs, conv_w, conv_b, dt_bias, a_log,
           d_skip, ssm_norm_w, w_proj_a, w_proj_b, w_out, final_norm_w):
    depth = w_in.shape[0]
    b, s, d = x.shape
    assert w_in.shape[2] == IN_END

    mod = _adaln_mod(c, ada_w, ada_b).reshape(depth, b, 3, 1, d)
    head_of_chan = jnp.arange(SSM_D_INNER, dtype=jnp.int32) // SSM_HEAD_DIM
    e1 = (jnp.arange(LANES, dtype=jnp.int32)[:, None] == head_of_chan[None, :]).astype(BF16)
    e2 = jnp.concatenate([e1, e1], axis=0)
    fnw = final_norm_w.reshape(1, d)

    lp = dict(
        nw=norm_w[:, None, :],
        w_bf=w_in.astype(BF16),
        w_gates=w_in[:, :, IN_GA:IN_END].astype(BF16),
        w_dt=jnp.pad(w_in[:, :, IN_DT:IN_GA], ((0, 0), (0, 0), (0, LANES - SSM_HEADS))).astype(BF16),
        lnw=gm_ln_w[:, None, :], lnb=gm_ln_b[:, None, :],
        ws=gm_ws,
        bsf=jnp.repeat(jnp.swapaxes(gm_bs, 1, 2), CHUNK, axis=2),
        convw=conv_w, convb=conv_b[:, None, :],
        dtb=_pad_lanes(dt_bias), alog=_pad_lanes(a_log),
        dskip=jnp.repeat(d_skip, SSM_HEAD_DIM, axis=1)[:, None, :],
        nrmw=ssm_norm_w[:, None, :],
        wpa=w_proj_a.astype(BF16), wpb=w_proj_b.astype(BF16), wo=w_out.astype(BF16),
    )
    for l in range(depth):
        pbg = _ssd_branch(x, mod, l, lp, e2)
        x = _gmlp_merge(x, mod, l, lp, pbg, fnw, tq=min(512, s), final_norm=(l == depth - 1))
    return x
```

```python
import functools

import jax
import jax.numpy as jnp
from jax import lax
from jax.experimental import pallas as pl
from jax.experimental.pallas import tpu as pltpu

F32 = jnp.float32
BF16 = jnp.bfloat16

D_MODEL = 1024
GM_WIDTH = 1024
GM_GROUPS = 8
CHUNK = 128
SSM_D_INNER = 2048
SSM_HEAD_DIM = 64
SSM_HEADS = 32
SSM_GROUPS = 8
HEADS_PER_GROUP = 4
GROUP_WIDTH = SSM_D_INNER // SSM_GROUPS
SSM_STATE = 128
BC_WIDTH = SSM_GROUPS * SSM_STATE
SSM_CONV = 4
CONV_DIM = SSM_D_INNER + 2 * BC_WIDTH
EPS = 1e-6

IN_U = 0
IN_SZ = 3 * GM_WIDTH
IN_XS = IN_SZ + SSM_D_INNER
IN_B = IN_XS + SSM_D_INNER
IN_C = IN_B + BC_WIDTH
IN_DT = IN_C + BC_WIDTH
IN_GA = IN_DT + SSM_HEADS
IN_GB = IN_GA + D_MODEL
IN_END = IN_GB + D_MODEL
W_BLOCK = 1024
assert all(off % W_BLOCK == 0 for off in (IN_SZ, IN_XS, IN_B, IN_C))

SSD_TILE = 2 * CHUNK
HALF_W = (CONV_DIM + SSM_D_INNER + D_MODEL) // 2
H0_GB = SSM_D_INNER + BC_WIDTH
H1_SZ, H1_GB = BC_WIDTH, BC_WIDTH + SSM_D_INNER
GB_HALF = D_MODEL // 2
PROJ_TILE = 256
PROJ_TILES = HALF_W // PROJ_TILE
STRIP = 256
N_STRIPS = CONV_DIM // STRIP
STRIPS_PER_PROJ = 3
PROJ_AFTER_CONV = N_STRIPS // STRIPS_PER_PROJ
PROJ_AFTER_DT = PROJ_AFTER_CONV + 1
assert PROJ_AFTER_DT + SSM_GROUPS == PROJ_TILES

LANES = 128
TAIL_ROWS = 16
XE_ROWS = TAIL_ROWS + CHUNK

VMEM_LIMIT_BYTES = 56 * 1024 * 1024


def _sigmoid(x):
    return 0.5 * jnp.tanh(0.5 * x) + 0.5


def _silu(x):
    hx = 0.5 * x
    return hx * jnp.tanh(hx) + hx


def _gelu_tanh(x):
    c = 0.7978845608028654
    hx = 0.5 * x
    return hx * jnp.tanh(x * (c + (0.044715 * c) * (x * x))) + hx


def _split3(v):
    hi = v.astype(BF16)
    r1 = v - hi.astype(F32)
    mid = r1.astype(BF16)
    lo = (r1 - mid.astype(F32)).astype(BF16)
    return hi, mid, lo


def _split2(v):
    hi = v.astype(BF16)
    lo = (v - hi.astype(F32)).astype(BF16)
    return hi, lo


def _adaln_rmsnorm(x, nw, scale, shift):
    y = x * lax.rsqrt(jnp.mean(x * x, axis=-1, keepdims=True) + EPS)
    return ((y * nw) * (1.0 + scale) + shift).astype(BF16)


def _adaln_kernel(c_ref, w_ref, b_ref, o_ref):
    c = c_ref[...]
    s = c * _sigmoid(c)
    w = w_ref[0]
    s_hi, s_lo = _split2(s)
    w_hi, w_lo = _split2(w)
    acc = jnp.dot(s_hi, w_hi, preferred_element_type=F32)
    acc = acc + jnp.dot(s_lo, w_hi, preferred_element_type=F32)
    acc = acc + jnp.dot(s_hi, w_lo, preferred_element_type=F32)
    o_ref[0] = acc + b_ref[0]


def _adaln_mod(c, ada_w, ada_b):
    depth, d, n3 = ada_w.shape
    b = c.shape[0]
    tn = 1024
    return pl.pallas_call(
        _adaln_kernel,
        out_shape=jax.ShapeDtypeStruct((depth, b, n3), F32),
        grid=(depth, n3 // tn),
        in_specs=[
            pl.BlockSpec((b, d), lambda l, n: (0, 0)),
            pl.BlockSpec((1, d, tn), lambda l, n: (l, 0, n)),
            pl.BlockSpec((1, 1, tn), lambda l, n: (l, 0, n)),
        ],
        out_specs=pl.BlockSpec((1, b, tn), lambda l, n: (l, 0, n)),
        compiler_params=pltpu.CompilerParams(dimension_semantics=("arbitrary", "arbitrary")),
        name="adaln_mod",
    )(c, ada_w, ada_b.reshape(depth, 1, n3))


def _resident(shape, layer=None):
    if layer is None:
        return pl.BlockSpec(shape, lambda *_: (0,) * len(shape), pipeline_mode=pl.Buffered(1))
    return pl.BlockSpec((None,) + shape, lambda *_: (layer,) + (0,) * len(shape), pipeline_mode=pl.Buffered(1))


def _w_block(layer, block, d, width=W_BLOCK):
    return pl.BlockSpec((None, d, width), lambda *_: (layer, 0, block), pipeline_mode=pl.Buffered(1))


MOD_SHIFT, MOD_SCALE, MOD_GATE = 0, 1, 2


def _mod_spec(layer, which, d, batch_of):
    return pl.BlockSpec((None, None, None, 1, d), lambda *idx: (layer, batch_of(*idx), which, 0, 0))


def _ssd_kernel(x0_ref, scale0_ref, shift0_ref, x_ref, nw_ref, scale_ref, shift_ref,
                wxs0_ref, wxs1_ref, wbm_ref, wcm_ref, wsz0_ref, wsz1_ref, wgb_ref,
                wdt_ref, convw_ref, convb_ref, dtb_ref, alog_ref, dskip_ref, nrmw_ref, wpb_ref, e2_ref, o_ref,
                ha_ref, hb_ref, pa_ref, pb_ref, dta_ref, dtb2_ref, state_ref, tail_ref, yb_ref, *, tiles_per_seq):
    t = pl.program_id(0)

    @pl.when(t == 0)
    def _():
        pb_ref[...] = jnp.zeros_like(pb_ref)
        dtb2_ref[...] = jnp.zeros_like(dtb2_ref)
        ha_ref[...] = _adaln_rmsnorm(x0_ref[0], nw_ref[...], scale0_ref[...], shift0_ref[...])

    @pl.when(jnp.logical_or(t == 0, lax.rem(t - 1, tiles_per_seq) == 0))
    def _():
        state_ref[...] = jnp.zeros_like(state_ref)
        tail_ref[...] = jnp.zeros_like(tail_ref)

    step = functools.partial(
        _ssd_step, x_ref, nw_ref, scale_ref, shift_ref,
        (wxs0_ref, wxs1_ref, wbm_ref, wcm_ref, wsz0_ref, wsz1_ref, wgb_ref), wdt_ref, convw_ref, convb_ref,
        dtb_ref, alog_ref, dskip_ref, nrmw_ref, wpb_ref, e2_ref, o_ref, state_ref, tail_ref, yb_ref)

    @pl.when(lax.rem(t, 2) == 0)
    def _():
        step(ha_ref, pa_ref, dta_ref, hb_ref, pb_ref, dtb2_ref)

    @pl.when(lax.rem(t, 2) == 1)
    def _():
        step(hb_ref, pb_ref, dtb2_ref, ha_ref, pa_ref, dta_ref)


def _ssd_step(x_ref, nw_ref, scale_ref, shift_ref, w_refs, wdt_ref, convw_ref, convb_ref, dtb_ref, alog_ref,
              dskip_ref, nrmw_ref, wpb_ref, e2_ref, o_ref, state_ref, tail_ref, yb_ref,
              h_ref, pw_ref, dtw_ref, hnext_ref, pr_ref, dtr_ref):
    wxs0, wxs1, wbm, wcm, wsz0, wsz1, wgb = w_refs
    per_block = W_BLOCK // PROJ_TILE
    block_tiles = lambda ref: [(ref, i) for i in range(per_block)]
    gb_tiles = GB_HALF // PROJ_TILE
    proj_sources = (
        block_tiles(wxs0) + block_tiles(wxs1) + block_tiles(wbm) + [(wgb, i) for i in range(gb_tiles)],
        block_tiles(wcm) + block_tiles(wsz0) + block_tiles(wsz1) + [(wgb, gb_tiles + i) for i in range(gb_tiles)],
    )
    assert all(len(srcs) == PROJ_TILES for srcs in proj_sources)
    dtw_ref[...] = jnp.dot(h_ref[...], wdt_ref[...], preferred_element_type=F32)

    row = lax.broadcasted_iota(jnp.int32, (CHUNK, CHUNK), 0)
    col = lax.broadcasted_iota(jnp.int32, (CHUNK, CHUNK), 1)
    tril = row >= col
    tri_ones = jnp.where(tril, 1.0, 0.0).astype(BF16)
    srow = lax.broadcasted_iota(jnp.int32, (CHUNK, (SSM_CONV - 1) * XE_ROWS), 0)
    scol = lax.broadcasted_iota(jnp.int32, (CHUNK, (SSM_CONV - 1) * XE_ROWS), 1)
    pick = functools.reduce(
        jnp.logical_or,
        [scol == (j - 1) * XE_ROWS + srow + (TAIL_ROWS - j) for j in range(1, SSM_CONV)])
    shift_sum = jnp.where(pick, 1.0, 0.0).astype(BF16)
    lane_blk = lax.broadcasted_iota(jnp.int32, (CHUNK, GROUP_WIDTH), 1) // SSM_HEAD_DIM
    head_half = lax.broadcasted_iota(jnp.int32, (CHUNK, LANES), 1) // SSM_HEAD_DIM
    a_neg = -jnp.exp(alog_ref[...])

    def chunk_body(c, carry):
        def project(k, after=None):
            src_ref, i = proj_sources[c][k]
            pw_ref[c, :, k * PROJ_TILE:(k + 1) * PROJ_TILE] = jnp.dot(
                h_ref[...], src_ref[:, i * PROJ_TILE:(i + 1) * PROJ_TILE],
                preferred_element_type=F32).astype(BF16)

        r0 = c * CHUNK
        rows = pl.ds(r0, CHUNK)

        acts = []
        for s in range(N_STRIPS):
            col0 = s * STRIP
            cols = slice(col0, col0 + STRIP)
            half, off = (0, col0) if col0 < H0_GB else (1, col0 - H0_GB)
            x_s = pr_ref[half, rows, off:off + STRIP]
            xe = jnp.concatenate([tail_ref[:, cols], x_s], axis=0)
            tail_ref[:, cols] = x_s[CHUNK - TAIL_ROWS:, :]
            wb = convw_ref[:, cols].astype(BF16)
            delayed = jnp.concatenate(
                [xe * wb[SSM_CONV - 1 - j:SSM_CONV - j, :] for j in range(1, SSM_CONV)], axis=0)
            acc = jnp.dot(shift_sum, delayed, preferred_element_type=F32)
            acc = acc + (x_s.astype(F32) * convw_ref[SSM_CONV - 1:SSM_CONV, cols] + convb_ref[:, cols])
            acts.append(_silu(acc))
            if s % STRIPS_PER_PROJ == STRIPS_PER_PROJ - 1:
                project(s // STRIPS_PER_PROJ, acts[-1])
        xs_strips = acts[:SSM_D_INNER // STRIP]
        bm_strips = acts[SSM_D_INNER // STRIP:(SSM_D_INNER + BC_WIDTH) // STRIP]
        cm_strips = acts[(SSM_D_INNER + BC_WIDTH) // STRIP:]

        dtv = dtr_ref[rows, :] + dtb_ref[...]
        dt = jnp.maximum(dtv, 0.0) + jnp.log(1.0 + jnp.exp(-jnp.abs(dtv)))
        adt = dt * a_neg
        hi, mid, lo = _split3(adt)
        cs3 = jnp.dot(tri_ones, jnp.concatenate([hi, mid, lo], axis=1), preferred_element_type=F32)
        a_cs = cs3[:, :LANES] + cs3[:, LANES:2 * LANES] + cs3[:, 2 * LANES:]
        a_cs_t = a_cs.T
        dt_t = dt.T
        w_end = dt * jnp.exp(a_cs[CHUNK - 1:CHUNK, :] - a_cs)
        we_hi, we_lo = _split2(w_end)
        we_cat = jnp.concatenate([we_hi, we_lo], axis=1)
        project(PROJ_AFTER_CONV)

        def per_channel(cols):
            return jnp.concatenate(
                [jnp.where(head_half == 0, cols[2 * i], cols[2 * i + 1]) for i in range(HEADS_PER_GROUP // 2)],
                axis=1)

        for g in range(SSM_GROUPS):
            gs = slice(g * GROUP_WIDTH, (g + 1) * GROUP_WIDTH)
            heads = range(g * HEADS_PER_GROUP, (g + 1) * HEADS_PER_GROUP)
            we_x = jnp.dot(we_cat, e2_ref[:, gs], preferred_element_type=F32)
            a_cols = [jnp.broadcast_to(a_cs[:, h:h + 1], (CHUNK, LANES)) for h in heads]
            ea_x = per_channel([jnp.exp(a_col) for a_col in a_cols])
            cd_x = ea_x[CHUNK - 1:CHUNK, :]
            bc_half = slice((g % 2) * SSM_STATE, (g % 2 + 1) * SSM_STATE)
            bm_g = bm_strips[g // 2][:, bc_half].astype(BF16)
            cm_g = cm_strips[g // 2][:, bc_half].astype(BF16)
            xs_g = xs_strips[g]
            xs_gb = xs_g.astype(BF16)
            cb = lax.dot_general(cm_g, bm_g, (((1,), (1,)), ((), ())), preferred_element_type=F32)
            ms = []
            for r, h in enumerate(heads):
                diff = a_cols[r] - a_cs_t[h:h + 1, :]
                decay = jnp.where(tril, jnp.exp(diff), 0.0)
                ms.append(((cb * decay) * dt_t[h:h + 1, :]).astype(BF16))
            lhs = jnp.concatenate(ms, axis=1)
            rhs = jnp.concatenate(
                [jnp.where(lane_blk == r, xs_gb, jnp.zeros_like(xs_gb)) for r in range(HEADS_PER_GROUP)],
                axis=0)
            y_diag = jnp.dot(lhs, rhs, preferred_element_type=F32)
            st = state_ref[:, gs]
            y_off = jnp.dot(cm_g, st.astype(BF16), preferred_element_type=F32) * ea_x
            xw = (xs_g * we_x).astype(BF16)
            new_st = lax.dot_general(bm_g, xw, (((0,), (0,)), ((), ())), preferred_element_type=F32)
            state_ref[:, gs] = st * cd_x + new_st
            project(PROJ_AFTER_DT + g, new_st)
            yb = y_diag + y_off + xs_g * dskip_ref[:, gs]
            sz = pr_ref[1, rows, H1_SZ + g * GROUP_WIDTH:H1_SZ + (g + 1) * GROUP_WIDTH].astype(F32)
            yz = yb * _silu(sz)
            inv = lax.rsqrt(jnp.mean(yz * yz, axis=-1, keepdims=True) + EPS)
            yb_ref[rows, gs] = ((yz * inv) * nrmw_ref[:, gs]).astype(BF16)

        pb = jnp.dot(yb_ref[rows, :], wpb_ref[...], preferred_element_type=F32)
        gb = jnp.concatenate([pr_ref[0, rows, H0_GB:H0_GB + GB_HALF],
                              pr_ref[1, rows, H1_GB:H1_GB + GB_HALF]], axis=1).astype(F32)
        o_ref[0, rows, :] = (_sigmoid(gb) * pb).astype(BF16)
        return carry

    for c in range(SSD_TILE // CHUNK):
        chunk_body(c, 0)

    hnext_ref[...] = _adaln_rmsnorm(x_ref[0], nw_ref[...], scale_ref[...], shift_ref[...])


def _ssd_branch(x, mod, layer, lp, e2):
    b, s, d = x.shape
    tq = SSD_TILE
    assert s % tq == 0
    nt = s // tq
    last = b * nt - 1

    def in_tile(t):
        tt = jnp.minimum(t + 1, last)
        return tt // nt, tt % nt

    def out_tile(t):
        tt = jnp.maximum(t - 1, 0)
        return tt // nt, tt % nt

    batch_in = lambda t: in_tile(t)[0]
    in_specs = [
        pl.BlockSpec((1, tq, d), lambda t: (0, 0, 0), pipeline_mode=pl.Buffered(1)),
        _mod_spec(layer, MOD_SCALE, d, lambda t: 0),
        _mod_spec(layer, MOD_SHIFT, d, lambda t: 0),
        pl.BlockSpec((1, tq, d), lambda t: (*in_tile(t), 0)),
        _resident((1, d), layer),
        _mod_spec(layer, MOD_SCALE, d, batch_in),
        _mod_spec(layer, MOD_SHIFT, d, batch_in),
        *[_w_block(layer, off // W_BLOCK, d) for off in
          (IN_XS, IN_XS + W_BLOCK, IN_B, IN_C, IN_SZ, IN_SZ + W_BLOCK)],
        _w_block(layer, 1, d),
        _resident((d, LANES), layer),
        _resident((SSM_CONV, CONV_DIM), layer), _resident((1, CONV_DIM), layer),
        _resident((1, LANES), layer), _resident((1, LANES), layer),
        _resident((1, SSM_D_INNER), layer), _resident((1, SSM_D_INNER), layer),
        _resident((SSM_D_INNER, d), layer), _resident((2 * LANES, SSM_D_INNER)),
    ]
    return pl.pallas_call(
        functools.partial(_ssd_kernel, tiles_per_seq=nt),
        out_shape=jax.ShapeDtypeStruct((b, s, d), BF16),
        grid=(b * nt + 1,),
        in_specs=in_specs,
        out_specs=pl.BlockSpec((1, tq, d), lambda t: (*out_tile(t), 0)),
        scratch_shapes=[
            pltpu.VMEM((tq, d), BF16),
            pltpu.VMEM((tq, d), BF16),
            pltpu.VMEM((2, tq, HALF_W), BF16),
            pltpu.VMEM((2, tq, HALF_W), BF16),
            pltpu.VMEM((tq, LANES), F32),
            pltpu.VMEM((tq, LANES), F32),
            pltpu.VMEM((SSM_STATE, SSM_D_INNER), F32),
            pltpu.VMEM((TAIL_ROWS, CONV_DIM), BF16),
            pltpu.VMEM((tq, SSM_D_INNER), BF16),
        ],
        compiler_params=pltpu.CompilerParams(
            dimension_semantics=("arbitrary",),
            vmem_limit_bytes=VMEM_LIMIT_BYTES),
        name="ssd_branch",
    )(x, mod, mod, x, lp["nw"], mod, mod, *([lp["w_bf"]] * 6), lp["w_gates"], lp["w_dt"], lp["convw"], lp["convb"],
      lp["dtb"], lp["alog"], lp["dskip"], lp["nrmw"], lp["wpb"], e2)


def _gmlp_merge_kernel(x_ref, nw_ref, scale_ref, shift_ref, gate_ref, wuvz_ref, wga_ref, lnw_ref, lnb_ref, ws_ref,
                       bsf_ref, wpa_ref, wo_ref, pbg_ref, fnw_ref, o_ref, *, tq, final_norm):
    x = x_ref[0]
    hb = _adaln_rmsnorm(x, nw_ref[...], scale_ref[...], shift_ref[...])
    p = jnp.dot(hb, wuvz_ref[...], preferred_element_type=F32)
    ga = jnp.dot(hb, wga_ref[...], preferred_element_type=F32)
    u = _gelu_tanh(p[:, 0:GM_WIDTH])
    v = _gelu_tanh(p[:, GM_WIDTH:2 * GM_WIDTH])
    mu = jnp.mean(v, axis=-1, keepdims=True)
    vc = v - mu
    var = jnp.mean(vc * vc, axis=-1, keepdims=True)
    vn = ((vc * lax.rsqrt(var + EPS)) * lnw_ref[...] + lnb_ref[...]).astype(BF16)

    row = lax.broadcasted_iota(jnp.int32, (CHUNK, CHUNK), 0)
    col = lax.broadcasted_iota(jnp.int32, (CHUNK, CHUNK), 1)
    tril = row >= col
    ws = [jnp.where(tril, ws_ref[g], 0.0).astype(BF16) for g in range(GM_GROUPS)]
    ws_pairs = [jnp.concatenate(ws[2 * i:2 * i + 2], axis=1) for i in range(GM_GROUPS // 2)]
    pair_half = lax.broadcasted_iota(jnp.int32, (CHUNK, 2 * CHUNK), 1) // CHUNK

    def mix_pair(i, k):
        v_pair = vn[k * CHUNK:(k + 1) * CHUNK, 2 * i * CHUNK:(2 * i + 2) * CHUNK]
        v_diag = jnp.concatenate(
            [jnp.where(pair_half == j, v_pair, jnp.zeros_like(v_pair)) for j in range(2)], axis=0)
        return jnp.dot(ws_pairs[i], v_diag, preferred_element_type=F32)

    mixed = jnp.concatenate([
        jnp.concatenate([mix_pair(i, k) for i in range(GM_GROUPS // 2)], axis=1) + bsf_ref[...]
        for k in range(tq // CHUNK)], axis=0)
    ya = ((u * mixed) * _silu(p[:, 2 * GM_WIDTH:3 * GM_WIDTH])).astype(BF16)
    pa = jnp.dot(ya, wpa_ref[...], preferred_element_type=F32)
    merged = (_sigmoid(ga) * pa + pbg_ref[0].astype(F32)).astype(BF16)
    out = x + gate_ref[...] * jnp.dot(merged, wo_ref[...], preferred_element_type=F32)
    if final_norm:
        out = (out * lax.rsqrt(jnp.mean(out * out, axis=-1, keepdims=True) + EPS)) * fnw_ref[...]
    o_ref[0] = out


def _gmlp_merge(x, mod, layer, lp, pbg, fnw, *, tq, final_norm):
    b, s, d = x.shape
    assert s % tq == 0
    tile = lambda i, j: (i, j, 0)
    batch_of = lambda i, j: i
    in_specs = [
        pl.BlockSpec((1, tq, d), tile),
        _resident((1, d), layer),
        _mod_spec(layer, MOD_SCALE, d, batch_of), _mod_spec(layer, MOD_SHIFT, d, batch_of),
        _mod_spec(layer, MOD_GATE, d, batch_of),
        _w_block(layer, 0, d, width=IN_SZ),
        _w_block(layer, 0, d),
        _resident((1, GM_WIDTH), layer), _resident((1, GM_WIDTH), layer),
        _resident((GM_GROUPS, CHUNK, CHUNK), layer), _resident((CHUNK, GM_WIDTH), layer),
        _resident((GM_WIDTH, d), layer), _resident((d, d), layer),
        pl.BlockSpec((1, tq, d), tile),
        _resident((1, d)),
    ]
    return pl.pallas_call(
        functools.partial(_gmlp_merge_kernel, tq=tq, final_norm=final_norm),
        out_shape=jax.ShapeDtypeStruct((b, s, d), F32),
        grid=(b, s // tq),
        in_specs=in_specs,
        out_specs=pl.BlockSpec((1, tq, d), tile),
        compiler_params=pltpu.CompilerParams(
            dimension_semantics=("arbitrary", "arbitrary"),
            vmem_limit_bytes=VMEM_LIMIT_BYTES),
        name="gmlp_merge",
    )(x, lp["nw"], mod, mod, mod, lp["w_bf"], lp["w_gates"], lp["lnw"], lp["lnb"], lp["ws"], lp["bsf"],
      lp["wpa"], lp["wo"], pbg, fnw)


def _pad_lanes(v):
    return jnp.pad(v, ((0, 0), (0, LANES - v.shape[1])))[:, None, :]


def kernel(x, c, ada_w, ada_b, norm_w, w_in, gm_ln_w, gm_ln_b, gm_ws, gm_bs, conv_w, conv_b, dt_bias, a_log,
           d_skip, ssm_norm_w, w_proj_a, w_proj_b, w_out, final_norm_w):
    depth = w_in.shape[0]
    b, s, d = x.shape
    assert w_in.shape[2] == IN_END

    mod = _adaln_mod(c, ada_w, ada_b).reshape(depth, b, 3, 1, d)
    fnw = final_norm_w.reshape(1, d)
    head_of_chan = jnp.arange(SSM_D_INNER, dtype=jnp.int32) // SSM_HEAD_DIM
    e1 = (jnp.arange(LANES, dtype=jnp.int32)[:, None] == head_of_chan[None, :]).astype(BF16)
    e2 = jnp.concatenate([e1, e1], axis=0)

    lp = dict(
        nw=norm_w[:, None, :],
        w_bf=w_in.astype(BF16),
        w_gates=w_in[:, :, IN_GA:IN_END].astype(BF16),
        w_dt=jnp.pad(w_in[:, :, IN_DT:IN_GA], ((0, 0), (0, 0), (0, LANES - SSM_HEADS))).astype(BF16),
        lnw=gm_ln_w[:, None, :], lnb=gm_ln_b[:, None, :],
        ws=gm_ws,
        bsf=jnp.repeat(jnp.swapaxes(gm_bs, 1, 2), CHUNK, axis=2),
        convw=conv_w, convb=conv_b[:, None, :],
        dtb=_pad_lanes(dt_bias), alog=_pad_lanes(a_log),
        dskip=jnp.repeat(d_skip, SSM_HEAD_DIM, axis=1)[:, None, :],
        nrmw=ssm_norm_w[:, None, :],
        wpa=w_proj_a.astype(BF16), wpb=w_proj_b.astype(BF16), wo=w_out.astype(BF16),
    )
    for l in range(depth):
        pbg = _ssd_branch(x, mod, l, lp, e2)
        x = _gmlp_merge(x, mod, l, lp, pbg, fnw, tq=min(512, s), final_norm=(l == depth - 1))
    return x
```

```python
import functools

import jax
import jax.numpy as jnp
from jax import lax
from jax.experimental import pallas as pl
from jax.experimental.pallas import tpu as pltpu

F32 = jnp.float32
BF16 = jnp.bfloat16

D_MODEL = 1024
GM_WIDTH = 1024
GM_GROUPS = 8
CHUNK = 128
SSM_D_INNER = 2048
SSM_HEAD_DIM = 64
SSM_HEADS = 32
SSM_GROUPS = 8
HEADS_PER_GROUP = 4
GROUP_WIDTH = SSM_D_INNER // SSM_GROUPS
SSM_STATE = 128
BC_WIDTH = SSM_GROUPS * SSM_STATE
SSM_CONV = 4
CONV_DIM = SSM_D_INNER + 2 * BC_WIDTH
EPS = 1e-6

IN_U = 0
IN_SZ = 3 * GM_WIDTH
IN_XS = IN_SZ + SSM_D_INNER
IN_B = IN_XS + SSM_D_INNER
IN_C = IN_B + BC_WIDTH
IN_DT = IN_C + BC_WIDTH
IN_GA = IN_DT + SSM_HEADS
IN_GB = IN_GA + D_MODEL
IN_END = IN_GB + D_MODEL
W_BLOCK = 1024
assert all(off % W_BLOCK == 0 for off in (IN_SZ, IN_XS, IN_B, IN_C))

SSD_TILE = 2 * CHUNK
HALF_W = (CONV_DIM + SSM_D_INNER + D_MODEL) // 2
H0_GB = SSM_D_INNER + BC_WIDTH
H1_SZ, H1_GB = BC_WIDTH, BC_WIDTH + SSM_D_INNER
GB_HALF = D_MODEL // 2
PROJ_TILE = 256
PROJ_TILES = HALF_W // PROJ_TILE
STRIP = 256
N_STRIPS = CONV_DIM // STRIP
STRIPS_PER_PROJ = 4
PROJ_AFTER_CONV = N_STRIPS // STRIPS_PER_PROJ
PROJ_AFTER_DT = PROJ_AFTER_CONV + 1
assert PROJ_AFTER_DT <= PROJ_TILES

LANES = 128
TAIL_ROWS = 16
XE_ROWS = TAIL_ROWS + CHUNK

VMEM_LIMIT_BYTES = 56 * 1024 * 1024


def _sigmoid(x):
    return 0.5 * jnp.tanh(0.5 * x) + 0.5


def _silu(x):
    hx = 0.5 * x
    return hx * jnp.tanh(hx) + hx


def _gelu_tanh(x):
    c = 0.7978845608028654
    hx = 0.5 * x
    return hx * jnp.tanh(x * (c + (0.044715 * c) * (x * x))) + hx


def _split3(v):
    hi = v.astype(BF16)
    r1 = v - hi.astype(F32)
    mid = r1.astype(BF16)
    lo = (r1 - mid.astype(F32)).astype(BF16)
    return hi, mid, lo


def _split2(v):
    hi = v.astype(BF16)
    lo = (v - hi.astype(F32)).astype(BF16)
    return hi, lo


def _adaln_rmsnorm(x, nw, scale, shift):
    y = x * lax.rsqrt(jnp.mean(x * x, axis=-1, keepdims=True) + EPS)
    return ((y * nw) * (1.0 + scale) + shift).astype(BF16)


def _adaln_kernel(c_ref, w_ref, b_ref, o_ref):
    c = c_ref[...]
    s = c * _sigmoid(c)
    w = w_ref[0]
    s_hi, s_lo = _split2(s)
    w_hi, w_lo = _split2(w)
    acc = jnp.dot(s_hi, w_hi, preferred_element_type=F32)
    acc = acc + jnp.dot(s_lo, w_hi, preferred_element_type=F32)
    acc = acc + jnp.dot(s_hi, w_lo, preferred_element_type=F32)
    o_ref[0] = acc + b_ref[0]


def _adaln_mod(c, ada_w, ada_b):
    depth, d, n3 = ada_w.shape
    b = c.shape[0]
    tn = 1024
    return pl.pallas_call(
        _adaln_kernel,
        out_shape=jax.ShapeDtypeStruct((depth, b, n3), F32),
        grid=(depth, n3 // tn),
        in_specs=[
            pl.BlockSpec((b, d), lambda l, n: (0, 0)),
            pl.BlockSpec((1, d, tn), lambda l, n: (l, 0, n)),
            pl.BlockSpec((1, 1, tn), lambda l, n: (l, 0, n)),
        ],
        out_specs=pl.BlockSpec((1, b, tn), lambda l, n: (l, 0, n)),
        compiler_params=pltpu.CompilerParams(dimension_semantics=("arbitrary", "arbitrary")),
        name="adaln_mod",
    )(c, ada_w, ada_b.reshape(depth, 1, n3))


def _resident(shape, layer=None):
    if layer is None:
        return pl.BlockSpec(shape, lambda *_: (0,) * len(shape), pipeline_mode=pl.Buffered(1))
    return pl.BlockSpec((None,) + shape, lambda *_: (layer,) + (0,) * len(shape), pipeline_mode=pl.Buffered(1))


def _w_block(layer, block, d, width=W_BLOCK):
    return pl.BlockSpec((None, d, width), lambda *_: (layer, 0, block), pipeline_mode=pl.Buffered(1))


MOD_SHIFT, MOD_SCALE, MOD_GATE = 0, 1, 2


def _mod_spec(layer, which, d, batch_of):
    return pl.BlockSpec((None, None, None, 1, d), lambda *idx: (layer, batch_of(*idx), which, 0, 0))


def _ssd_kernel(x0_ref, scale0_ref, shift0_ref, x_ref, nw_ref, scale_ref, shift_ref,
                wxs0_ref, wxs1_ref, wbm_ref, wcm_ref, wsz0_ref, wsz1_ref, wgb_ref,
                wdt_ref, convw_ref, convb_ref, dtb_ref, alog_ref, dskip_ref, nrmw_ref, wpb_ref, e2_ref, o_ref,
                ha_ref, hb_ref, pa_ref, pb_ref, dta_ref, dtb2_ref, state_ref, tail_ref, yb_ref, *, tiles_per_seq):
    t = pl.program_id(0)

    @pl.when(t == 0)
    def _():
        pb_ref[...] = jnp.zeros_like(pb_ref)
        dtb2_ref[...] = jnp.zeros_like(dtb2_ref)
        ha_ref[...] = _adaln_rmsnorm(x0_ref[0], nw_ref[...], scale0_ref[...], shift0_ref[...])

    @pl.when(jnp.logical_or(t == 0, lax.rem(t - 1, tiles_per_seq) == 0))
    def _():
        state_ref[...] = jnp.zeros_like(state_ref)
        tail_ref[...] = jnp.zeros_like(tail_ref)

    step = functools.partial(
        _ssd_step, x_ref, nw_ref, scale_ref, shift_ref,
        (wxs0_ref, wxs1_ref, wbm_ref, wcm_ref, wsz0_ref, wsz1_ref, wgb_ref), wdt_ref, convw_ref, convb_ref,
        dtb_ref, alog_ref, dskip_ref, nrmw_ref, wpb_ref, e2_ref, o_ref, state_ref, tail_ref, yb_ref)

    @pl.when(lax.rem(t, 2) == 0)
    def _():
        step(ha_ref, pa_ref, dta_ref, hb_ref, pb_ref, dtb2_ref)

    @pl.when(lax.rem(t, 2) == 1)
    def _():
        step(hb_ref, pb_ref, dtb2_ref, ha_ref, pa_ref, dta_ref)


def _ssd_step(x_ref, nw_ref, scale_ref, shift_ref, w_refs, wdt_ref, convw_ref, convb_ref, dtb_ref, alog_ref,
              dskip_ref, nrmw_ref, wpb_ref, e2_ref, o_ref, state_ref, tail_ref, yb_ref,
              h_ref, pw_ref, dtw_ref, hnext_ref, pr_ref, dtr_ref):
    wxs0, wxs1, wbm, wcm, wsz0, wsz1, wgb = w_refs
    per_block = W_BLOCK // PROJ_TILE
    block_tiles = lambda ref: [(ref, i) for i in range(per_block)]
    gb_tiles = GB_HALF // PROJ_TILE
    proj_sources = (
        block_tiles(wxs0) + block_tiles(wxs1) + block_tiles(wbm) + [(wgb, i) for i in range(gb_tiles)],
        block_tiles(wcm) + block_tiles(wsz0) + block_tiles(wsz1) + [(wgb, gb_tiles + i) for i in range(gb_tiles)],
    )
    assert all(len(srcs) == PROJ_TILES for srcs in proj_sources)
    dtw_ref[...] = jnp.dot(h_ref[...], wdt_ref[...], preferred_element_type=F32)

    row = lax.broadcasted_iota(jnp.int32, (CHUNK, CHUNK), 0)
    col = lax.broadcasted_iota(jnp.int32, (CHUNK, CHUNK), 1)
    tril = row >= col
    tri_ones = jnp.where(tril, 1.0, 0.0).astype(BF16)
    srow = lax.broadcasted_iota(jnp.int32, (CHUNK, (SSM_CONV - 1) * XE_ROWS), 0)
    scol = lax.broadcasted_iota(jnp.int32, (CHUNK, (SSM_CONV - 1) * XE_ROWS), 1)
    pick = functools.reduce(
        jnp.logical_or,
        [scol == (j - 1) * XE_ROWS + srow + (TAIL_ROWS - j) for j in range(1, SSM_CONV)])
    shift_sum = jnp.where(pick, 1.0, 0.0).astype(BF16)
    lane_blk = lax.broadcasted_iota(jnp.int32, (CHUNK, GROUP_WIDTH), 1) // SSM_HEAD_DIM
    head_half = lax.broadcasted_iota(jnp.int32, (CHUNK, LANES), 1) // SSM_HEAD_DIM
    a_neg = -jnp.exp(alog_ref[...])

    def chunk_body(c, carry):
        def project(k, after=None):
            src_ref, i = proj_sources[c][k]
            pw_ref[c, :, k * PROJ_TILE:(k + 1) * PROJ_TILE] = jnp.dot(
                h_ref[...], src_ref[:, i * PROJ_TILE:(i + 1) * PROJ_TILE],
                preferred_element_type=F32).astype(BF16)

        r0 = c * CHUNK
        rows = pl.ds(r0, CHUNK)

        acts = []
        for s in range(N_STRIPS):
            col0 = s * STRIP
            cols = slice(col0, col0 + STRIP)
            half, off = (0, col0) if col0 < H0_GB else (1, col0 - H0_GB)
            x_s = pr_ref[half, rows, off:off + STRIP]
            xe = jnp.concatenate([tail_ref[:, cols], x_s], axis=0)
            tail_ref[:, cols] = x_s[CHUNK - TAIL_ROWS:, :]
            wb = convw_ref[:, cols].astype(BF16)
            delayed = jnp.concatenate(
                [xe * wb[SSM_CONV - 1 - j:SSM_CONV - j, :] for j in range(1, SSM_CONV)], axis=0)
            acc = jnp.dot(shift_sum, delayed, preferred_element_type=F32)
            acc = acc + (x_s.astype(F32) * convw_ref[SSM_CONV - 1:SSM_CONV, cols] + convb_ref[:, cols])
            acts.append(_silu(acc))
            if s % STRIPS_PER_PROJ == STRIPS_PER_PROJ - 1:
                project(s // STRIPS_PER_PROJ, acts[-1])
        xs_strips = acts[:SSM_D_INNER // STRIP]
        bm_strips = acts[SSM_D_INNER // STRIP:(SSM_D_INNER + BC_WIDTH) // STRIP]
        cm_strips = acts[(SSM_D_INNER + BC_WIDTH) // STRIP:]

        dtv = dtr_ref[rows, :] + dtb_ref[...]
        dt = jnp.maximum(dtv, 0.0) + jnp.log(1.0 + jnp.exp(-jnp.abs(dtv)))
        adt = dt * a_neg
        hi, mid, lo = _split3(adt)
        cs3 = jnp.dot(tri_ones, jnp.concatenate([hi, mid, lo], axis=1), preferred_element_type=F32)
        a_cs = cs3[:, :LANES] + cs3[:, LANES:2 * LANES] + cs3[:, 2 * LANES:]
        a_cs_t = a_cs.T
        dt_t = dt.T
        w_end = dt * jnp.exp(a_cs[CHUNK - 1:CHUNK, :] - a_cs)
        we_hi, we_lo = _split2(w_end)
        we_cat = jnp.concatenate([we_hi, we_lo], axis=1)
        project(PROJ_AFTER_CONV)

        def per_channel(cols):
            return jnp.concatenate(
                [jnp.where(head_half == 0, cols[2 * i], cols[2 * i + 1]) for i in range(HEADS_PER_GROUP // 2)],
                axis=1)

        for g in range(SSM_GROUPS):
            gs = slice(g * GROUP_WIDTH, (g + 1) * GROUP_WIDTH)
            heads = range(g * HEADS_PER_GROUP, (g + 1) * HEADS_PER_GROUP)
            we_x = jnp.dot(we_cat, e2_ref[:, gs], preferred_element_type=F32)
            a_cols = [jnp.broadcast_to(a_cs[:, h:h + 1], (CHUNK, LANES)) for h in heads]
            ea_x = per_channel([jnp.exp(a_col) for a_col in a_cols])
            cd_x = ea_x[CHUNK - 1:CHUNK, :]
            bc_half = slice((g % 2) * SSM_STATE, (g % 2 + 1) * SSM_STATE)
            bm_g = bm_strips[g // 2][:, bc_half].astype(BF16)
            cm_g = cm_strips[g // 2][:, bc_half].astype(BF16)
            xs_g = xs_strips[g]
            xs_gb = xs_g.astype(BF16)
            cb = lax.dot_general(cm_g, bm_g, (((1,), (1,)), ((), ())), preferred_element_type=F32)
            ms = []
            for r, h in enumerate(heads):
                diff = a_cols[r] - a_cs_t[h:h + 1, :]
                decay = jnp.where(tril, jnp.exp(diff), 0.0)
                ms.append(((cb * decay) * dt_t[h:h + 1, :]).astype(BF16))
            lhs = jnp.concatenate(ms, axis=1)
            rhs = jnp.concatenate(
                [jnp.where(lane_blk == r, xs_gb, jnp.zeros_like(xs_gb)) for r in range(HEADS_PER_GROUP)],
                axis=0)
            y_diag = jnp.dot(lhs, rhs, preferred_element_type=F32)
            st = state_ref[:, gs]
            y_off = jnp.dot(cm_g, st.astype(BF16), preferred_element_type=F32) * ea_x
            xw = (xs_g * we_x).astype(BF16)
            new_st = lax.dot_general(bm_g, xw, (((0,), (0,)), ((), ())), preferred_element_type=F32)
            state_ref[:, gs] = st * cd_x + new_st
            if PROJ_AFTER_DT + g < PROJ_TILES:
                project(PROJ_AFTER_DT + g)
            yb = y_diag + y_off + xs_g * dskip_ref[:, gs]
            sz = pr_ref[1, rows, H1_SZ + g * GROUP_WIDTH:H1_SZ + (g + 1) * GROUP_WIDTH].astype(F32)
            yz = yb * _silu(sz)
            inv = lax.rsqrt(jnp.mean(yz * yz, axis=-1, keepdims=True) + EPS)
            yb_ref[rows, gs] = ((yz * inv) * nrmw_ref[:, gs]).astype(BF16)

        for k in range(PROJ_AFTER_DT + SSM_GROUPS, PROJ_TILES):
            project(k)

        pb = jnp.dot(yb_ref[rows, :], wpb_ref[...], preferred_element_type=F32)
        gb = jnp.concatenate([pr_ref[0, rows, H0_GB:H0_GB + GB_HALF],
                              pr_ref[1, rows, H1_GB:H1_GB + GB_HALF]], axis=1).astype(F32)
        o_ref[0, rows, :] = (_sigmoid(gb) * pb).astype(BF16)
        return carry

    for c in range(SSD_TILE // CHUNK):
        chunk_body(c, 0)

    hnext_ref[...] = _adaln_rmsnorm(x_ref[0], nw_ref[...], scale_ref[...], shift_ref[...])


def _ssd_branch(x, mod, layer, lp, e2):
    b, s, d = x.shape
    tq = SSD_TILE
    assert s % tq == 0
    nt = s // tq
    last = b * nt - 1

    def in_tile(t):
        tt = jnp.minimum(t + 1, last)
        return tt // nt, tt % nt

    def out_tile(t):
        tt = jnp.maximum(t - 1, 0)
        return tt // nt, tt % nt

    batch_in = lambda t: in_tile(t)[0]
    in_specs = [
        pl.BlockSpec((1, tq, d), lambda t: (0, 0, 0), pipeline_mode=pl.Buffered(1)),
        _mod_spec(layer, MOD_SCALE, d, lambda t: 0),
        _mod_spec(layer, MOD_SHIFT, d, lambda t: 0),
        pl.BlockSpec((1, tq, d), lambda t: (*in_tile(t), 0)),
        _resident((1, d), layer),
        _mod_spec(layer, MOD_SCALE, d, batch_in),
        _mod_spec(layer, MOD_SHIFT, d, batch_in),
        *[_w_block(layer, off // W_BLOCK, d) for off in
          (IN_XS, IN_XS + W_BLOCK, IN_B, IN_C, IN_SZ, IN_SZ + W_BLOCK)],
        _w_block(layer, 1, d),
        _resident((d, LANES), layer),
        _resident((SSM_CONV, CONV_DIM), layer), _resident((1, CONV_DIM), layer),
        _resident((1, LANES), layer), _resident((1, LANES), layer),
        _resident((1, SSM_D_INNER), layer), _resident((1, SSM_D_INNER), layer),
        _resident((SSM_D_INNER, d), layer), _resident((2 * LANES, SSM_D_INNER)),
    ]
    return pl.pallas_call(
        functools.partial(_ssd_kernel, tiles_per_seq=nt),
        out_shape=jax.ShapeDtypeStruct((b, s, d), BF16),
        grid=(b * nt + 1,),
        in_specs=in_specs,
        out_specs=pl.BlockSpec((1, tq, d), lambda t: (*out_tile(t), 0)),
        scratch_shapes=[
            pltpu.VMEM((tq, d), BF16),
            pltpu.VMEM((tq, d), BF16),
            pltpu.VMEM((2, tq, HALF_W), BF16),
            pltpu.VMEM((2, tq, HALF_W), BF16),
            pltpu.VMEM((tq, LANES), F32),
            pltpu.VMEM((tq, LANES), F32),
            pltpu.VMEM((SSM_STATE, SSM_D_INNER), F32),
            pltpu.VMEM((TAIL_ROWS, CONV_DIM), BF16),
            pltpu.VMEM((tq, SSM_D_INNER), BF16),
        ],
        compiler_params=pltpu.CompilerParams(
            dimension_semantics=("arbitrary",),
            vmem_limit_bytes=VMEM_LIMIT_BYTES),
        name="ssd_branch",
    )(x, mod, mod, x, lp["nw"], mod, mod, *([lp["w_bf"]] * 6), lp["w_gates"], lp["w_dt"], lp["convw"], lp["convb"],
      lp["dtb"], lp["alog"], lp["dskip"], lp["nrmw"], lp["wpb"], e2)


def _gmlp_merge_kernel(x_ref, nw_ref, scale_ref, shift_ref, gate_ref, wuvz_ref, wga_ref, lnw_ref, lnb_ref, ws_ref,
                       bsf_ref, wpa_ref, wo_ref, pbg_ref, fnw_ref, o_ref, *, tq, final_norm):
    x = x_ref[0]
    hb = _adaln_rmsnorm(x, nw_ref[...], scale_ref[...], shift_ref[...])
    p = jnp.dot(hb, wuvz_ref[...], preferred_element_type=F32)
    ga = jnp.dot(hb, wga_ref[...], preferred_element_type=F32)
    u = _gelu_tanh(p[:, 0:GM_WIDTH])
    v = _gelu_tanh(p[:, GM_WIDTH:2 * GM_WIDTH])
    mu = jnp.mean(v, axis=-1, keepdims=True)
    vc = v - mu
    var = jnp.mean(vc * vc, axis=-1, keepdims=True)
    vn = ((vc * lax.rsqrt(var + EPS)) * lnw_ref[...] + lnb_ref[...]).astype(BF16)

    row = lax.broadcasted_iota(jnp.int32, (CHUNK, CHUNK), 0)
    col = lax.broadcasted_iota(jnp.int32, (CHUNK, CHUNK), 1)
    tril = row >= col
    ws = [jnp.where(tril, ws_ref[g], 0.0).astype(BF16) for g in range(GM_GROUPS)]
    ws_pairs = [jnp.concatenate(ws[2 * i:2 * i + 2], axis=1) for i in range(GM_GROUPS // 2)]
    pair_half = lax.broadcasted_iota(jnp.int32, (CHUNK, 2 * CHUNK), 1) // CHUNK

    def mix_pair(i, k):
        v_pair = vn[k * CHUNK:(k + 1) * CHUNK, 2 * i * CHUNK:(2 * i + 2) * CHUNK]
        v_diag = jnp.concatenate(
            [jnp.where(pair_half == j, v_pair, jnp.zeros_like(v_pair)) for j in range(2)], axis=0)
        return jnp.dot(ws_pairs[i], v_diag, preferred_element_type=F32)

    mixed = jnp.concatenate([
        jnp.concatenate([mix_pair(i, k) for i in range(GM_GROUPS // 2)], axis=1) + bsf_ref[...]
        for k in range(tq // CHUNK)], axis=0)
    ya = ((u * mixed) * _silu(p[:, 2 * GM_WIDTH:3 * GM_WIDTH])).astype(BF16)
    pa = jnp.dot(ya, wpa_ref[...], preferred_element_type=F32)
    merged = (_sigmoid(ga) * pa + pbg_ref[0].astype(F32)).astype(BF16)
    out = x + gate_ref[...] * jnp.dot(merged, wo_ref[...], preferred_element_type=F32)
    if final_norm:
        out = (out * lax.rsqrt(jnp.mean(out * out, axis=-1, keepdims=True) + EPS)) * fnw_ref[...]
    o_ref[0] = out


def _gmlp_merge(x, mod, layer, lp, pbg, fnw, *, tq, final_norm):
    b, s, d = x.shape
    assert s % tq == 0
    tile = lambda i, j: (i, j, 0)
    batch_of = lambda i, j: i
    in_specs = [
        pl.BlockSpec((1, tq, d), tile),
        _resident((1, d), layer),
        _mod_spec(layer, MOD_SCALE, d, batch_of), _mod_spec(layer, MOD_SHIFT, d, batch_of),
        _mod_spec(layer, MOD_GATE, d, batch_of),
        _w_block(layer, 0, d, width=IN_SZ),
        _w_block(layer, 0, d),
        _resident((1, GM_WIDTH), layer), _resident((1, GM_WIDTH), layer),
        _resident((GM_GROUPS, CHUNK, CHUNK), layer), _resident((CHUNK, GM_WIDTH), layer),
        _resident((GM_WIDTH, d), layer), _resident((d, d), layer),
        pl.BlockSpec((1, tq, d), tile),
        _resident((1, d)),
    ]
    return pl.pallas_call(
        functools.partial(_gmlp_merge_kernel, tq=tq, final_norm=final_norm),
        out_shape=jax.ShapeDtypeStruct((b, s, d), F32),
        grid=(b, s // tq),
        in_specs=in_specs,
        out_specs=pl.BlockSpec((1, tq, d), tile),
        compiler_params=pltpu.CompilerParams(
            dimension_semantics=("arbitrary", "arbitrary"),
            vmem_limit_bytes=VMEM_LIMIT_BYTES),
        name="gmlp_merge",
    )(x, lp["nw"], mod, mod, mod, lp["w_bf"], lp["w_gates"], lp["lnw"], lp["lnb"], lp["ws"], lp["bsf"],
      lp["wpa"], lp["wo"], pbg, fnw)


def _pad_lanes(v):
    return jnp.pad(v, ((0, 0), (0, LANES - v.shape[1])))[:, None, :]


def kernel(x, c, ada_w, ada_b, norm_w, w_in, gm_ln_w, gm_ln_b, gm_ws, gm_bs, conv_w, conv_b, dt_bias, a_log,
           d_skip, ssm_norm_w, w_proj_a, w_proj_b, w_out, final_norm_w):
    depth = w_in.shape[0]
    b, s, d = x.shape
    assert w_in.shape[2] == IN_END

    mod = _adaln_mod(c, ada_w, ada_b).reshape(depth, b, 3, 1, d)
    fnw = final_norm_w.reshape(1, d)
    head_of_chan = jnp.arange(SSM_D_INNER, dtype=jnp.int32) // SSM_HEAD_DIM
    e1 = (jnp.arange(LANES, dtype=jnp.int32)[:, None] == head_of_chan[None, :]).astype(BF16)
    e2 = jnp.concatenate([e1, e1], axis=0)

    lp = dict(
        nw=norm_w[:, None, :],
        w_bf=w_in.astype(BF16),
        w_gates=w_in[:, :, IN_GA:IN_END].astype(BF16),
        w_dt=jnp.pad(w_in[:, :, IN_DT:IN_GA], ((0, 0), (0, 0), (0, LANES - SSM_HEADS))).astype(BF16),
        lnw=gm_ln_w[:, None, :], lnb=gm_ln_b[:, None, :],
        ws=gm_ws,
        bsf=jnp.repeat(jnp.swapaxes(gm_bs, 1, 2), CHUNK, axis=2),
        convw=conv_w, convb=conv_b[:, None, :],
        dtb=_pad_lanes(dt_bias), alog=_pad_lanes(a_log),
        dskip=jnp.repeat(d_skip, SSM_HEAD_DIM, axis=1)[:, None, :],
        nrmw=ssm_norm_w[:, None, :],
        wpa=w_proj_a.astype(BF16), wpb=w_proj_b.astype(BF16), wo=w_out.astype(BF16),
    )
    for l in range(depth):
        pbg = _ssd_branch(x, mod, l, lp, e2)
        x = _gmlp_merge(x, mod, l, lp, pbg, fnw, tq=min(512, s), final_norm=(l == depth - 1))
    return x
```

```python
import functools

import jax
import jax.numpy as jnp
from jax import lax
from jax.experimental import pallas as pl
from jax.experimental.pallas import tpu as pltpu

F32 = jnp.float32
BF16 = jnp.bfloat16

D_MODEL = 1024
GM_WIDTH = 1024
GM_GROUPS = 8
CHUNK = 128
SSM_D_INNER = 2048
SSM_HEAD_DIM = 64
SSM_HEADS = 32
SSM_GROUPS = 8
HEADS_PER_GROUP = 4
GROUP_WIDTH = SSM_D_INNER // SSM_GROUPS
SSM_STATE = 128
BC_WIDTH = SSM_GROUPS * SSM_STATE
SSM_CONV = 4
CONV_DIM = SSM_D_INNER + 2 * BC_WIDTH
EPS = 1e-6

IN_U = 0
IN_SZ = 3 * GM_WIDTH
IN_XS = IN_SZ + SSM_D_INNER
IN_B = IN_XS + SSM_D_INNER
IN_C = IN_B + BC_WIDTH
IN_DT = IN_C + BC_WIDTH
IN_GA = IN_DT + SSM_HEADS
IN_GB = IN_GA + D_MODEL
IN_END = IN_GB + D_MODEL
W_BLOCK = 1024
assert all(off % W_BLOCK == 0 for off in (IN_SZ, IN_XS, IN_B, IN_C))

SSD_TILE = 2 * CHUNK
HALF_W = (CONV_DIM + SSM_D_INNER + D_MODEL) // 2
H0_GB = SSM_D_INNER + BC_WIDTH
H1_SZ, H1_GB = BC_WIDTH, BC_WIDTH + SSM_D_INNER
GB_HALF = D_MODEL // 2
PROJ_TILE = 256
PROJ_TILES = HALF_W // PROJ_TILE
STRIP = 256
N_STRIPS = CONV_DIM // STRIP
STRIPS_PER_PROJ = 4
PROJ_AFTER_CONV = N_STRIPS // STRIPS_PER_PROJ
PROJ_AFTER_DT = PROJ_AFTER_CONV + 1
assert PROJ_AFTER_DT <= PROJ_TILES

LANES = 128
TAIL_ROWS = 16
XE_ROWS = TAIL_ROWS + CHUNK

VMEM_LIMIT_BYTES = 56 * 1024 * 1024


def _sigmoid(x):
    return 0.5 * jnp.tanh(0.5 * x) + 0.5


def _silu(x):
    hx = 0.5 * x
    return hx * jnp.tanh(hx) + hx


def _gelu_tanh(x):
    c = 0.7978845608028654
    hx = 0.5 * x
    return hx * jnp.tanh(x * (c + (0.044715 * c) * (x * x))) + hx


def _split3(v):
    hi = v.astype(BF16)
    r1 = v - hi.astype(F32)
    mid = r1.astype(BF16)
    lo = (r1 - mid.astype(F32)).astype(BF16)
    return hi, mid, lo


def _split2(v):
    hi = v.astype(BF16)
    lo = (v - hi.astype(F32)).astype(BF16)
    return hi, lo


def _adaln_rmsnorm(x, nw, scale, shift):
    y = x * lax.rsqrt(jnp.mean(x * x, axis=-1, keepdims=True) + EPS)
    return ((y * nw) * (1.0 + scale) + shift).astype(BF16)


def _adaln_kernel(c_ref, w_ref, b_ref, o_ref):
    c = c_ref[...]
    s = c * _sigmoid(c)
    w = w_ref[0]
    s_hi, s_lo = _split2(s)
    w_hi, w_lo = _split2(w)
    acc = jnp.dot(s_hi, w_hi, preferred_element_type=F32)
    acc = acc + jnp.dot(s_lo, w_hi, preferred_element_type=F32)
    acc = acc + jnp.dot(s_hi, w_lo, preferred_element_type=F32)
    o_ref[0] = acc + b_ref[0]


def _adaln_mod(c, ada_w, ada_b):
    depth, d, n3 = ada_w.shape
    b = c.shape[0]
    tn = 1024
    return pl.pallas_call(
        _adaln_kernel,
        out_shape=jax.ShapeDtypeStruct((depth, b, n3), F32),
        grid=(depth, n3 // tn),
        in_specs=[
            pl.BlockSpec((b, d), lambda l, n: (0, 0)),
            pl.BlockSpec((1, d, tn), lambda l, n: (l, 0, n)),
            pl.BlockSpec((1, 1, tn), lambda l, n: (l, 0, n)),
        ],
        out_specs=pl.BlockSpec((1, b, tn), lambda l, n: (l, 0, n)),
        compiler_params=pltpu.CompilerParams(dimension_semantics=("arbitrary", "arbitrary")),
        name="adaln_mod",
    )(c, ada_w, ada_b.reshape(depth, 1, n3))


def _resident(shape, layer=None):
    if layer is None:
        return pl.BlockSpec(shape, lambda *_: (0,) * len(shape), pipeline_mode=pl.Buffered(1))
    return pl.BlockSpec((None,) + shape, lambda *_: (layer,) + (0,) * len(shape), pipeline_mode=pl.Buffered(1))


def _w_block(layer, block, d, width=W_BLOCK):
    return pl.BlockSpec((None, d, width), lambda *_: (layer, 0, block), pipeline_mode=pl.Buffered(1))


MOD_SHIFT, MOD_SCALE, MOD_GATE = 0, 1, 2


def _mod_spec(layer, which, d, batch_of):
    return pl.BlockSpec((None, None, None, 1, d), lambda *idx: (layer, batch_of(*idx), which, 0, 0))


def _ssd_kernel(x0_ref, scale0_ref, shift0_ref, x_ref, nw_ref, scale_ref, shift_ref,
                wxs0_ref, wxs1_ref, wbm_ref, wcm_ref, wsz0_ref, wsz1_ref, wgb_ref,
                wdt_ref, convw_ref, convb_ref, dtb_ref, alog_ref, dskip_ref, nrmw_ref, wpb_ref, e2_ref, o_ref,
                ha_ref, hb_ref, pa_ref, pb_ref, dta_ref, dtb2_ref, state_ref, tail_ref, yb_ref, *, tiles_per_seq):
    t = pl.program_id(0)

    @pl.when(t == 0)
    def _():
        pb_ref[...] = jnp.zeros_like(pb_ref)
        dtb2_ref[...] = jnp.zeros_like(dtb2_ref)
        ha_ref[...] = _adaln_rmsnorm(x0_ref[0], nw_ref[...], scale0_ref[...], shift0_ref[...])

    @pl.when(jnp.logical_or(t == 0, lax.rem(t - 1, tiles_per_seq) == 0))
    def _():
        state_ref[...] = jnp.zeros_like(state_ref)
        tail_ref[...] = jnp.zeros_like(tail_ref)

    step = functools.partial(
        _ssd_step, x_ref, nw_ref, scale_ref, shift_ref,
        (wxs0_ref, wxs1_ref, wbm_ref, wcm_ref, wsz0_ref, wsz1_ref, wgb_ref), wdt_ref, convw_ref, convb_ref,
        dtb_ref, alog_ref, dskip_ref, nrmw_ref, wpb_ref, e2_ref, o_ref, state_ref, tail_ref, yb_ref)

    @pl.when(lax.rem(t, 2) == 0)
    def _():
        step(ha_ref, pa_ref, dta_ref, hb_ref, pb_ref, dtb2_ref)

    @pl.when(lax.rem(t, 2) == 1)
    def _():
        step(hb_ref, pb_ref, dtb2_ref, ha_ref, pa_ref, dta_ref)


def _ssd_step(x_ref, nw_ref, scale_ref, shift_ref, w_refs, wdt_ref, convw_ref, convb_ref, dtb_ref, alog_ref,
              dskip_ref, nrmw_ref, wpb_ref, e2_ref, o_ref, state_ref, tail_ref, yb_ref,
              h_ref, pw_ref, dtw_ref, hnext_ref, pr_ref, dtr_ref):
    wxs0, wxs1, wbm, wcm, wsz0, wsz1, wgb = w_refs
    per_block = W_BLOCK // PROJ_TILE
    block_tiles = lambda ref: [(ref, i) for i in range(per_block)]
    gb_tiles = GB_HALF // PROJ_TILE
    proj_sources = (
        block_tiles(wxs0) + block_tiles(wxs1) + block_tiles(wbm) + [(wgb, i) for i in range(gb_tiles)],
        block_tiles(wcm) + block_tiles(wsz0) + block_tiles(wsz1) + [(wgb, gb_tiles + i) for i in range(gb_tiles)],
    )
    assert all(len(srcs) == PROJ_TILES for srcs in proj_sources)
    dtw_ref[...] = jnp.dot(h_ref[...], wdt_ref[...], preferred_element_type=F32)

    row = lax.broadcasted_iota(jnp.int32, (CHUNK, CHUNK), 0)
    col = lax.broadcasted_iota(jnp.int32, (CHUNK, CHUNK), 1)
    tril = row >= col
    tri_ones = jnp.where(tril, 1.0, 0.0).astype(BF16)
    srow = lax.broadcasted_iota(jnp.int32, (CHUNK, (SSM_CONV - 1) * XE_ROWS), 0)
    scol = lax.broadcasted_iota(jnp.int32, (CHUNK, (SSM_CONV - 1) * XE_ROWS), 1)
    pick = functools.reduce(
        jnp.logical_or,
        [scol == (j - 1) * XE_ROWS + srow + (TAIL_ROWS - j) for j in range(1, SSM_CONV)])
    shift_sum = jnp.where(pick, 1.0, 0.0).astype(BF16)
    lane_blk = lax.broadcasted_iota(jnp.int32, (CHUNK, GROUP_WIDTH), 1) // SSM_HEAD_DIM
    head_half = lax.broadcasted_iota(jnp.int32, (CHUNK, LANES), 1) // SSM_HEAD_DIM
    a_neg = -jnp.exp(alog_ref[...])

    def project(c, k):
        src_ref, i = proj_sources[c][k]
        pw_ref[c, :, k * PROJ_TILE:(k + 1) * PROJ_TILE] = jnp.dot(
            h_ref[...], src_ref[:, i * PROJ_TILE:(i + 1) * PROJ_TILE],
            preferred_element_type=F32).astype(BF16)

    def conv_strip(c, s):
        rows = pl.ds(c * CHUNK, CHUNK)
        col0 = s * STRIP
        cols = slice(col0, col0 + STRIP)
        half, off = (0, col0) if col0 < H0_GB else (1, col0 - H0_GB)
        x_s = pr_ref[half, rows, off:off + STRIP]
        xe = jnp.concatenate([tail_ref[:, cols], x_s], axis=0)
        tail_ref[:, cols] = x_s[CHUNK - TAIL_ROWS:, :]
        wb = convw_ref[:, cols].astype(BF16)
        delayed = jnp.concatenate(
            [xe * wb[SSM_CONV - 1 - j:SSM_CONV - j, :] for j in range(1, SSM_CONV)], axis=0)
        acc = jnp.dot(shift_sum, delayed, preferred_element_type=F32)
        acc = acc + (x_s.astype(F32) * convw_ref[SSM_CONV - 1:SSM_CONV, cols] + convb_ref[:, cols])
        act = _silu(acc)
        if s % STRIPS_PER_PROJ == STRIPS_PER_PROJ - 1:
            project(c, s // STRIPS_PER_PROJ)
        return act

    def decay_terms(c):
        rows = pl.ds(c * CHUNK, CHUNK)
        dtv = dtr_ref[rows, :] + dtb_ref[...]
        dt = jnp.maximum(dtv, 0.0) + jnp.log(1.0 + jnp.exp(-jnp.abs(dtv)))
        adt = dt * a_neg
        hi, mid, lo = _split3(adt)
        cs3 = jnp.dot(tri_ones, jnp.concatenate([hi, mid, lo], axis=1), preferred_element_type=F32)
        a_cs = cs3[:, :LANES] + cs3[:, LANES:2 * LANES] + cs3[:, 2 * LANES:]
        a_cs_t = a_cs.T
        dt_t = dt.T
        w_end = dt * jnp.exp(a_cs[CHUNK - 1:CHUNK, :] - a_cs)
        we_hi, we_lo = _split2(w_end)
        we_cat = jnp.concatenate([we_hi, we_lo], axis=1)
        project(c, PROJ_AFTER_CONV)
        return a_cs, a_cs_t, dt_t, we_cat

    def per_channel(cols):
        return jnp.concatenate(
            [jnp.where(head_half == 0, cols[2 * i], cols[2 * i + 1]) for i in range(HEADS_PER_GROUP // 2)],
            axis=1)

    def scan(c, acts, terms):
        a_cs, a_cs_t, dt_t, we_cat = terms
        rows = pl.ds(c * CHUNK, CHUNK)
        xs_strips = acts[:SSM_D_INNER // STRIP]
        bm_strips = acts[SSM_D_INNER // STRIP:(SSM_D_INNER + BC_WIDTH) // STRIP]
        cm_strips = acts[(SSM_D_INNER + BC_WIDTH) // STRIP:]

        for g in range(SSM_GROUPS):
            gs = slice(g * GROUP_WIDTH, (g + 1) * GROUP_WIDTH)
            heads = range(g * HEADS_PER_GROUP, (g + 1) * HEADS_PER_GROUP)
            we_x = jnp.dot(we_cat, e2_ref[:, gs], preferred_element_type=F32)
            a_cols = [jnp.broadcast_to(a_cs[:, h:h + 1], (CHUNK, LANES)) for h in heads]
            ea_x = per_channel([jnp.exp(a_col) for a_col in a_cols])
            cd_x = ea_x[CHUNK - 1:CHUNK, :]
            bc_half = slice((g % 2) * SSM_STATE, (g % 2 + 1) * SSM_STATE)
            bm_g = bm_strips[g // 2][:, bc_half].astype(BF16)
            cm_g = cm_strips[g // 2][:, bc_half].astype(BF16)
            xs_g = xs_strips[g]
            xs_gb = xs_g.astype(BF16)
            cb = lax.dot_general(cm_g, bm_g, (((1,), (1,)), ((), ())), preferred_element_type=F32)
            ms = []
            for r, h in enumerate(heads):
                diff = a_cols[r] - a_cs_t[h:h + 1, :]
                decay = jnp.where(tril, jnp.exp(diff), 0.0)
                ms.append(((cb * decay) * dt_t[h:h + 1, :]).astype(BF16))
            lhs = jnp.concatenate(ms, axis=1)
            rhs = jnp.concatenate(
                [jnp.where(lane_blk == r, xs_gb, jnp.zeros_like(xs_gb)) for r in range(HEADS_PER_GROUP)],
                axis=0)
            y_diag = jnp.dot(lhs, rhs, preferred_element_type=F32)
            st = state_ref[:, gs]
            y_off = jnp.dot(cm_g, st.astype(BF16), preferred_element_type=F32) * ea_x
            xw = (xs_g * we_x).astype(BF16)
            new_st = lax.dot_general(bm_g, xw, (((0,), (0,)), ((), ())), preferred_element_type=F32)
            state_ref[:, gs] = st * cd_x + new_st
            if PROJ_AFTER_DT + g < PROJ_TILES:
                project(c, PROJ_AFTER_DT + g)
            yb = y_diag + y_off + xs_g * dskip_ref[:, gs]
            sz = pr_ref[1, rows, H1_SZ + g * GROUP_WIDTH:H1_SZ + (g + 1) * GROUP_WIDTH].astype(F32)
            yz = yb * _silu(sz)
            inv = lax.rsqrt(jnp.mean(yz * yz, axis=-1, keepdims=True) + EPS)
            yb_ref[rows, gs] = ((yz * inv) * nrmw_ref[:, gs]).astype(BF16)

        for k in range(PROJ_AFTER_DT + SSM_GROUPS, PROJ_TILES):
            project(c, k)

        pb = jnp.dot(yb_ref[rows, :], wpb_ref[...], preferred_element_type=F32)
        gb = jnp.concatenate([pr_ref[0, rows, H0_GB:H0_GB + GB_HALF],
                              pr_ref[1, rows, H1_GB:H1_GB + GB_HALF]], axis=1).astype(F32)
        o_ref[0, rows, :] = (_sigmoid(gb) * pb).astype(BF16)

    for c in range(SSD_TILE // CHUNK):
        acts = [conv_strip(c, s) for s in range(N_STRIPS)]
        scan(c, acts, decay_terms(c))

    hnext_ref[...] = _adaln_rmsnorm(x_ref[0], nw_ref[...], scale_ref[...], shift_ref[...])


def _ssd_branch(x, mod, layer, lp, e2):
    b, s, d = x.shape
    tq = SSD_TILE
    assert s % tq == 0
    nt = s // tq
    last = b * nt - 1

    def in_tile(t):
        tt = jnp.minimum(t + 1, last)
        return tt // nt, tt % nt

    def out_tile(t):
        tt = jnp.maximum(t - 1, 0)
        return tt // nt, tt % nt

    batch_in = lambda t: in_tile(t)[0]
    in_specs = [
        pl.BlockSpec((1, tq, d), lambda t: (0, 0, 0), pipeline_mode=pl.Buffered(1)),
        _mod_spec(layer, MOD_SCALE, d, lambda t: 0),
        _mod_spec(layer, MOD_SHIFT, d, lambda t: 0),
        pl.BlockSpec((1, tq, d), lambda t: (*in_tile(t), 0)),
        _resident((1, d), layer),
        _mod_spec(layer, MOD_SCALE, d, batch_in),
        _mod_spec(layer, MOD_SHIFT, d, batch_in),
        *[_w_block(layer, off // W_BLOCK, d) for off in
          (IN_XS, IN_XS + W_BLOCK, IN_B, IN_C, IN_SZ, IN_SZ + W_BLOCK)],
        _w_block(layer, 1, d),
        _resident((d, LANES), layer),
        _resident((SSM_CONV, CONV_DIM), layer), _resident((1, CONV_DIM), layer),
        _resident((1, LANES), layer), _resident((1, LANES), layer),
        _resident((1, SSM_D_INNER), layer), _resident((1, SSM_D_INNER), layer),
        _resident((SSM_D_INNER, d), layer), _resident((2 * LANES, SSM_D_INNER)),
    ]
    return pl.pallas_call(
        functools.partial(_ssd_kernel, tiles_per_seq=nt),
        out_shape=jax.ShapeDtypeStruct((b, s, d), BF16),
        grid=(b * nt + 1,),
        in_specs=in_specs,
        out_specs=pl.BlockSpec((1, tq, d), lambda t: (*out_tile(t), 0)),
        scratch_shapes=[
            pltpu.VMEM((tq, d), BF16),
            pltpu.VMEM((tq, d), BF16),
            pltpu.VMEM((2, tq, HALF_W), BF16),
            pltpu.VMEM((2, tq, HALF_W), BF16),
            pltpu.VMEM((tq, LANES), F32),
            pltpu.VMEM((tq, LANES), F32),
            pltpu.VMEM((SSM_STATE, SSM_D_INNER), F32),
            pltpu.VMEM((TAIL_ROWS, CONV_DIM), BF16),
            pltpu.VMEM((tq, SSM_D_INNER), BF16),
        ],
        compiler_params=pltpu.CompilerParams(
            dimension_semantics=("arbitrary",),
            vmem_limit_bytes=VMEM_LIMIT_BYTES),
        name="ssd_branch",
    )(x, mod, mod, x, lp["nw"], mod, mod, *([lp["w_bf"]] * 6), lp["w_gates"], lp["w_dt"], lp["convw"], lp["convb"],
      lp["dtb"], lp["alog"], lp["dskip"], lp["nrmw"], lp["wpb"], e2)


def _gmlp_merge_kernel(x_ref, nw_ref, scale_ref, shift_ref, gate_ref, wuvz_ref, wga_ref, lnw_ref, lnb_ref, ws_ref,
                       bsf_ref, wpa_ref, wo_ref, pbg_ref, fnw_ref, o_ref, *, tq, final_norm):
    x = x_ref[0]
    hb = _adaln_rmsnorm(x, nw_ref[...], scale_ref[...], shift_ref[...])
    p = jnp.dot(hb, wuvz_ref[...], preferred_element_type=F32)
    ga = jnp.dot(hb, wga_ref[...], preferred_element_type=F32)
    u = _gelu_tanh(p[:, 0:GM_WIDTH])
    v = _gelu_tanh(p[:, GM_WIDTH:2 * GM_WIDTH])
    mu = jnp.mean(v, axis=-1, keepdims=True)
    vc = v - mu
    var = jnp.mean(vc * vc, axis=-1, keepdims=True)
    vn = ((vc * lax.rsqrt(var + EPS)) * lnw_ref[...] + lnb_ref[...]).astype(BF16)

    row = lax.broadcasted_iota(jnp.int32, (CHUNK, CHUNK), 0)
    col = lax.broadcasted_iota(jnp.int32, (CHUNK, CHUNK), 1)
    tril = row >= col
    ws = [jnp.where(tril, ws_ref[g], 0.0).astype(BF16) for g in range(GM_GROUPS)]
    ws_pairs = [jnp.concatenate(ws[2 * i:2 * i + 2], axis=1) for i in range(GM_GROUPS // 2)]
    pair_half = lax.broadcasted_iota(jnp.int32, (CHUNK, 2 * CHUNK), 1) // CHUNK

    def mix_pair(i, k):
        v_pair = vn[k * CHUNK:(k + 1) * CHUNK, 2 * i * CHUNK:(2 * i + 2) * CHUNK]
        v_diag = jnp.concatenate(
            [jnp.where(pair_half == j, v_pair, jnp.zeros_like(v_pair)) for j in range(2)], axis=0)
        return jnp.dot(ws_pairs[i], v_diag, preferred_element_type=F32)

    mixed = jnp.concatenate([
        jnp.concatenate([mix_pair(i, k) for i in range(GM_GROUPS // 2)], axis=1) + bsf_ref[...]
        for k in range(tq // CHUNK)], axis=0)
    ya = ((u * mixed) * _silu(p[:, 2 * GM_WIDTH:3 * GM_WIDTH])).astype(BF16)
    pa = jnp.dot(ya, wpa_ref[...], preferred_element_type=F32)
    merged = (_sigmoid(ga) * pa + pbg_ref[0].astype(F32)).astype(BF16)
    out = x + gate_ref[...] * jnp.dot(merged, wo_ref[...], preferred_element_type=F32)
    if final_norm:
        out = (out * lax.rsqrt(jnp.mean(out * out, axis=-1, keepdims=True) + EPS)) * fnw_ref[...]
    o_ref[0] = out


def _gmlp_merge(x, mod, layer, lp, pbg, fnw, *, tq, final_norm):
    b, s, d = x.shape
    assert s % tq == 0
    tile = lambda i, j: (i, j, 0)
    batch_of = lambda i, j: i
    in_specs = [
        pl.BlockSpec((1, tq, d), tile),
        _resident((1, d), layer),
        _mod_spec(layer, MOD_SCALE, d, batch_of), _mod_spec(layer, MOD_SHIFT, d, batch_of),
        _mod_spec(layer, MOD_GATE, d, batch_of),
        _w_block(layer, 0, d, width=IN_SZ),
        _w_block(layer, 0, d),
        _resident((1, GM_WIDTH), layer), _resident((1, GM_WIDTH), layer),
        _resident((GM_GROUPS, CHUNK, CHUNK), layer), _resident((CHUNK, GM_WIDTH), layer),
        _resident((GM_WIDTH, d), layer), _resident((d, d), layer),
        pl.BlockSpec((1, tq, d), tile),
        _resident((1, d)),
    ]
    return pl.pallas_call(
        functools.partial(_gmlp_merge_kernel, tq=tq, final_norm=final_norm),
        out_shape=jax.ShapeDtypeStruct((b, s, d), F32),
        grid=(b, s // tq),
        in_specs=in_specs,
        out_specs=pl.BlockSpec((1, tq, d), tile),
        compiler_params=pltpu.CompilerParams(
            dimension_semantics=("arbitrary", "arbitrary"),
            vmem_limit_bytes=VMEM_LIMIT_BYTES),
        name="gmlp_merge",
    )(x, lp["nw"], mod, mod, mod, lp["w_bf"], lp["w_gates"], lp["lnw"], lp["lnb"], lp["ws"], lp["bsf"],
      lp["wpa"], lp["wo"], pbg, fnw)


def _pad_lanes(v):
    return jnp.pad(v, ((0, 0), (0, LANES - v.shape[1])))[:, None, :]


def kernel(x, c, ada_w, ada_b, norm_w, w_in, gm_ln_w, gm_ln_b, gm_ws, gm_bs, conv_w, conv_b, dt_bias, a_log,
           d_skip, ssm_norm_w, w_proj_a, w_proj_b, w_out, final_norm_w):
    depth = w_in.shape[0]
    b, s, d = x.shape
    assert w_in.shape[2] == IN_END

    mod = _adaln_mod(c, ada_w, ada_b).reshape(depth, b, 3, 1, d)
    fnw = final_norm_w.reshape(1, d)
    head_of_chan = jnp.arange(SSM_D_INNER, dtype=jnp.int32) // SSM_HEAD_DIM
    e1 = (jnp.arange(LANES, dtype=jnp.int32)[:, None] == head_of_chan[None, :]).astype(BF16)
    e2 = jnp.concatenate([e1, e1], axis=0)

    lp = dict(
        nw=norm_w[:, None, :],
        w_bf=w_in.astype(BF16),
        w_gates=w_in[:, :, IN_GA:IN_END].astype(BF16),
        w_dt=jnp.pad(w_in[:, :, IN_DT:IN_GA], ((0, 0), (0, 0), (0, LANES - SSM_HEADS))).astype(BF16),
        lnw=gm_ln_w[:, None, :], lnb=gm_ln_b[:, None, :],
        ws=gm_ws,
        bsf=jnp.repeat(jnp.swapaxes(gm_bs, 1, 2), CHUNK, axis=2),
        convw=conv_w, convb=conv_b[:, None, :],
        dtb=_pad_lanes(dt_bias), alog=_pad_lanes(a_log),
        dskip=jnp.repeat(d_skip, SSM_HEAD_DIM, axis=1)[:, None, :],
        nrmw=ssm_norm_w[:, None, :],
        wpa=w_proj_a.astype(BF16), wpb=w_proj_b.astype(BF16), wo=w_out.astype(BF16),
    )
    for l in range(depth):
        pbg = _ssd_branch(x, mod, l, lp, e2)
        x = _gmlp_merge(x, mod, l, lp, pbg, fnw, tq=min(512, s), final_norm=(l == depth - 1))
    return x
```

```python
import functools

import jax
import jax.numpy as jnp
from jax import lax
from jax.experimental import pallas as pl
from jax.experimental.pallas import tpu as pltpu

F32 = jnp.float32
BF16 = jnp.bfloat16

D_MODEL = 1024
GM_WIDTH = 1024
GM_GROUPS = 8
CHUNK = 128
SSM_D_INNER = 2048
SSM_HEAD_DIM = 64
SSM_HEADS = 32
SSM_GROUPS = 8
HEADS_PER_GROUP = 4
GROUP_WIDTH = SSM_D_INNER // SSM_GROUPS
SSM_STATE = 128
BC_WIDTH = SSM_GROUPS * SSM_STATE
SSM_CONV = 4
CONV_DIM = SSM_D_INNER + 2 * BC_WIDTH
EPS = 1e-6

IN_U = 0
IN_SZ = 3 * GM_WIDTH
IN_XS = IN_SZ + SSM_D_INNER
IN_B = IN_XS + SSM_D_INNER
IN_C = IN_B + BC_WIDTH
IN_DT = IN_C + BC_WIDTH
IN_GA = IN_DT + SSM_HEADS
IN_GB = IN_GA + D_MODEL
IN_END = IN_GB + D_MODEL
W_BLOCK = 1024
assert all(off % W_BLOCK == 0 for off in (IN_SZ, IN_XS, IN_B, IN_C))

SSD_TILE = 2 * CHUNK
HALF_W = (CONV_DIM + SSM_D_INNER + D_MODEL) // 2
H0_GB = SSM_D_INNER + BC_WIDTH
H1_SZ, H1_GB = BC_WIDTH, BC_WIDTH + SSM_D_INNER
GB_HALF = D_MODEL // 2
PROJ_TILE = 256
PROJ_TILES = HALF_W // PROJ_TILE
STRIP = 256
N_STRIPS = CONV_DIM // STRIP
STRIPS_PER_PROJ = 4
PROJ_AFTER_CONV = N_STRIPS // STRIPS_PER_PROJ
PROJ_AFTER_DT = PROJ_AFTER_CONV + 1
assert PROJ_AFTER_DT <= PROJ_TILES

LANES = 128
TAIL_ROWS = 16
XE_ROWS = TAIL_ROWS + CHUNK

VMEM_LIMIT_BYTES = 56 * 1024 * 1024


def _sigmoid(x):
    return 0.5 * jnp.tanh(0.5 * x) + 0.5


def _silu(x):
    hx = 0.5 * x
    return hx * jnp.tanh(hx) + hx


def _gelu_tanh(x):
    c = 0.7978845608028654
    hx = 0.5 * x
    return hx * jnp.tanh(x * (c + (0.044715 * c) * (x * x))) + hx


def _split3(v):
    hi = v.astype(BF16)
    r1 = v - hi.astype(F32)
    mid = r1.astype(BF16)
    lo = (r1 - mid.astype(F32)).astype(BF16)
    return hi, mid, lo


def _split2(v):
    hi = v.astype(BF16)
    lo = (v - hi.astype(F32)).astype(BF16)
    return hi, lo


def _adaln_rmsnorm(x, nw, scale, shift):
    y = x * lax.rsqrt(jnp.mean(x * x, axis=-1, keepdims=True) + EPS)
    return ((y * nw) * (1.0 + scale) + shift).astype(BF16)


def _adaln_kernel(c_ref, w_ref, b_ref, o_ref):
    c = c_ref[...]
    s = c * _sigmoid(c)
    w = w_ref[0]
    s_hi, s_lo = _split2(s)
    w_hi, w_lo = _split2(w)
    acc = jnp.dot(s_hi, w_hi, preferred_element_type=F32)
    acc = acc + jnp.dot(s_lo, w_hi, preferred_element_type=F32)
    acc = acc + jnp.dot(s_hi, w_lo, preferred_element_type=F32)
    o_ref[0] = acc + b_ref[0]


def _adaln_mod(c, ada_w, ada_b):
    depth, d, n3 = ada_w.shape
    b = c.shape[0]
    tn = 1024
    return pl.pallas_call(
        _adaln_kernel,
        out_shape=jax.ShapeDtypeStruct((depth, b, n3), F32),
        grid=(depth, n3 // tn),
        in_specs=[
            pl.BlockSpec((b, d), lambda l, n: (0, 0)),
            pl.BlockSpec((1, d, tn), lambda l, n: (l, 0, n)),
            pl.BlockSpec((1, 1, tn), lambda l, n: (l, 0, n)),
        ],
        out_specs=pl.BlockSpec((1, b, tn), lambda l, n: (l, 0, n)),
        compiler_params=pltpu.CompilerParams(dimension_semantics=("arbitrary", "arbitrary")),
        name="adaln_mod",
    )(c, ada_w, ada_b.reshape(depth, 1, n3))


def _resident(shape, layer=None):
    if layer is None:
        return pl.BlockSpec(shape, lambda *_: (0,) * len(shape), pipeline_mode=pl.Buffered(1))
    return pl.BlockSpec((None,) + shape, lambda *_: (layer,) + (0,) * len(shape), pipeline_mode=pl.Buffered(1))


def _w_block(layer, block, d, width=W_BLOCK):
    return pl.BlockSpec((None, d, width), lambda *_: (layer, 0, block), pipeline_mode=pl.Buffered(1))


MOD_SHIFT, MOD_SCALE, MOD_GATE = 0, 1, 2


def _mod_spec(layer, which, d, batch_of):
    return pl.BlockSpec((None, None, None, 1, d), lambda *idx: (layer, batch_of(*idx), which, 0, 0))


def _ssd_kernel(x0_ref, scale0_ref, shift0_ref, x_ref, nw_ref, scale_ref, shift_ref,
                wxs0_ref, wxs1_ref, wbm_ref, wcm_ref, wsz0_ref, wsz1_ref, wgb_ref,
                wdt_ref, convw_ref, convb_ref, dtb_ref, alog_ref, dskip_ref, nrmw_ref, e2_ref, oy_ref, og_ref,
                ha_ref, hb_ref, pa_ref, pb_ref, dta_ref, dtb2_ref, state_ref, tail_ref, *, tiles_per_seq):
    t = pl.program_id(0)

    @pl.when(t == 0)
    def _():
        pb_ref[...] = jnp.zeros_like(pb_ref)
        dtb2_ref[...] = jnp.zeros_like(dtb2_ref)
        ha_ref[...] = _adaln_rmsnorm(x0_ref[0], nw_ref[...], scale0_ref[...], shift0_ref[...])

    @pl.when(jnp.logical_or(t == 0, lax.rem(t - 1, tiles_per_seq) == 0))
    def _():
        state_ref[...] = jnp.zeros_like(state_ref)
        tail_ref[...] = jnp.zeros_like(tail_ref)

    step = functools.partial(
        _ssd_step, x_ref, nw_ref, scale_ref, shift_ref,
        (wxs0_ref, wxs1_ref, wbm_ref, wcm_ref, wsz0_ref, wsz1_ref, wgb_ref), wdt_ref, convw_ref, convb_ref,
        dtb_ref, alog_ref, dskip_ref, nrmw_ref, e2_ref, oy_ref, og_ref, state_ref, tail_ref)

    @pl.when(lax.rem(t, 2) == 0)
    def _():
        step(ha_ref, pa_ref, dta_ref, hb_ref, pb_ref, dtb2_ref)

    @pl.when(lax.rem(t, 2) == 1)
    def _():
        step(hb_ref, pb_ref, dtb2_ref, ha_ref, pa_ref, dta_ref)


def _ssd_step(x_ref, nw_ref, scale_ref, shift_ref, w_refs, wdt_ref, convw_ref, convb_ref, dtb_ref, alog_ref,
              dskip_ref, nrmw_ref, e2_ref, oy_ref, og_ref, state_ref, tail_ref,
              h_ref, pw_ref, dtw_ref, hnext_ref, pr_ref, dtr_ref):
    wxs0, wxs1, wbm, wcm, wsz0, wsz1, wgb = w_refs
    per_block = W_BLOCK // PROJ_TILE
    block_tiles = lambda ref: [(ref, i) for i in range(per_block)]
    gb_tiles = GB_HALF // PROJ_TILE
    proj_sources = (
        block_tiles(wxs0) + block_tiles(wxs1) + block_tiles(wbm) + [(wgb, i) for i in range(gb_tiles)],
        block_tiles(wcm) + block_tiles(wsz0) + block_tiles(wsz1) + [(wgb, gb_tiles + i) for i in range(gb_tiles)],
    )
    assert all(len(srcs) == PROJ_TILES for srcs in proj_sources)
    dtw_ref[...] = jnp.dot(h_ref[...], wdt_ref[...], preferred_element_type=F32)

    row = lax.broadcasted_iota(jnp.int32, (CHUNK, CHUNK), 0)
    col = lax.broadcasted_iota(jnp.int32, (CHUNK, CHUNK), 1)
    tril = row >= col
    tri_ones = jnp.where(tril, 1.0, 0.0).astype(BF16)
    srow = lax.broadcasted_iota(jnp.int32, (CHUNK, (SSM_CONV - 1) * XE_ROWS), 0)
    scol = lax.broadcasted_iota(jnp.int32, (CHUNK, (SSM_CONV - 1) * XE_ROWS), 1)
    pick = functools.reduce(
        jnp.logical_or,
        [scol == (j - 1) * XE_ROWS + srow + (TAIL_ROWS - j) for j in range(1, SSM_CONV)])
    shift_sum = jnp.where(pick, 1.0, 0.0).astype(BF16)
    lane_blk = lax.broadcasted_iota(jnp.int32, (CHUNK, GROUP_WIDTH), 1) // SSM_HEAD_DIM
    head_half = lax.broadcasted_iota(jnp.int32, (CHUNK, LANES), 1) // SSM_HEAD_DIM
    a_neg = -jnp.exp(alog_ref[...])

    def project(c, k):
        src_ref, i = proj_sources[c][k]
        pw_ref[c, :, k * PROJ_TILE:(k + 1) * PROJ_TILE] = jnp.dot(
            h_ref[...], src_ref[:, i * PROJ_TILE:(i + 1) * PROJ_TILE],
            preferred_element_type=F32).astype(BF16)

    def conv_strip(c, s):
        rows = pl.ds(c * CHUNK, CHUNK)
        col0 = s * STRIP
        cols = slice(col0, col0 + STRIP)
        half, off = (0, col0) if col0 < H0_GB else (1, col0 - H0_GB)
        x_s = pr_ref[half, rows, off:off + STRIP]
        xe = jnp.concatenate([tail_ref[:, cols], x_s], axis=0)
        tail_ref[:, cols] = x_s[CHUNK - TAIL_ROWS:, :]
        wb = convw_ref[:, cols].astype(BF16)
        delayed = jnp.concatenate(
            [xe * wb[SSM_CONV - 1 - j:SSM_CONV - j, :] for j in range(1, SSM_CONV)], axis=0)
        acc = jnp.dot(shift_sum, delayed, preferred_element_type=F32)
        acc = acc + (x_s.astype(F32) * convw_ref[SSM_CONV - 1:SSM_CONV, cols] + convb_ref[:, cols])
        act = _silu(acc)
        if s % STRIPS_PER_PROJ == STRIPS_PER_PROJ - 1:
            project(c, s // STRIPS_PER_PROJ)
        return act

    def decay_terms(c):
        rows = pl.ds(c * CHUNK, CHUNK)
        dtv = dtr_ref[rows, :] + dtb_ref[...]
        dt = jnp.maximum(dtv, 0.0) + jnp.log(1.0 + jnp.exp(-jnp.abs(dtv)))
        adt = dt * a_neg
        hi, mid, lo = _split3(adt)
        cs3 = jnp.dot(tri_ones, jnp.concatenate([hi, mid, lo], axis=1), preferred_element_type=F32)
        a_cs = cs3[:, :LANES] + cs3[:, LANES:2 * LANES] + cs3[:, 2 * LANES:]
        a_cs_t = a_cs.T
        dt_t = dt.T
        w_end = dt * jnp.exp(a_cs[CHUNK - 1:CHUNK, :] - a_cs)
        we_hi, we_lo = _split2(w_end)
        we_cat = jnp.concatenate([we_hi, we_lo], axis=1)
        project(c, PROJ_AFTER_CONV)
        return a_cs, a_cs_t, dt_t, we_cat

    def per_channel(cols):
        return jnp.concatenate(
            [jnp.where(head_half == 0, cols[2 * i], cols[2 * i + 1]) for i in range(HEADS_PER_GROUP // 2)],
            axis=1)

    def scan(c, acts, terms):
        a_cs, a_cs_t, dt_t, we_cat = terms
        rows = pl.ds(c * CHUNK, CHUNK)
        xs_strips = acts[:SSM_D_INNER // STRIP]
        bm_strips = acts[SSM_D_INNER // STRIP:(SSM_D_INNER + BC_WIDTH) // STRIP]
        cm_strips = acts[(SSM_D_INNER + BC_WIDTH) // STRIP:]

        for g in range(SSM_GROUPS):
            gs = slice(g * GROUP_WIDTH, (g + 1) * GROUP_WIDTH)
            heads = range(g * HEADS_PER_GROUP, (g + 1) * HEADS_PER_GROUP)
            we_x = jnp.dot(we_cat, e2_ref[:, gs], preferred_element_type=F32)
            a_cols = [jnp.broadcast_to(a_cs[:, h:h + 1], (CHUNK, LANES)) for h in heads]
            ea_x = per_channel([jnp.exp(a_col) for a_col in a_cols])
            cd_x = ea_x[CHUNK - 1:CHUNK, :]
            bc_half = slice((g % 2) * SSM_STATE, (g % 2 + 1) * SSM_STATE)
            bm_g = bm_strips[g // 2][:, bc_half].astype(BF16)
            cm_g = cm_strips[g // 2][:, bc_half].astype(BF16)
            xs_g = xs_strips[g]
            xs_gb = xs_g.astype(BF16)
            cb = lax.dot_general(cm_g, bm_g, (((1,), (1,)), ((), ())), preferred_element_type=F32)
            ms = []
            for r, h in enumerate(heads):
                diff = a_cols[r] - a_cs_t[h:h + 1, :]
                decay = jnp.where(tril, jnp.exp(diff), 0.0)
                ms.append(((cb * decay) * dt_t[h:h + 1, :]).astype(BF16))
            lhs = jnp.concatenate(ms, axis=1)
            rhs = jnp.concatenate(
                [jnp.where(lane_blk == r, xs_gb, jnp.zeros_like(xs_gb)) for r in range(HEADS_PER_GROUP)],
                axis=0)
            y_diag = jnp.dot(lhs, rhs, preferred_element_type=F32)
            st = state_ref[:, gs]
            y_off = jnp.dot(cm_g, st.astype(BF16), preferred_element_type=F32) * ea_x
            xw = (xs_g * we_x).astype(BF16)
            new_st = lax.dot_general(bm_g, xw, (((0,), (0,)), ((), ())), preferred_element_type=F32)
            state_ref[:, gs] = st * cd_x + new_st
            if PROJ_AFTER_DT + g < PROJ_TILES:
                project(c, PROJ_AFTER_DT + g)
            yb = y_diag + y_off + xs_g * dskip_ref[:, gs]
            sz = pr_ref[1, rows, H1_SZ + g * GROUP_WIDTH:H1_SZ + (g + 1) * GROUP_WIDTH].astype(F32)
            yz = yb * _silu(sz)
            inv = lax.rsqrt(jnp.mean(yz * yz, axis=-1, keepdims=True) + EPS)
            oy_ref[0, rows, gs] = ((yz * inv) * nrmw_ref[:, gs]).astype(BF16)

        for k in range(PROJ_AFTER_DT + SSM_GROUPS, PROJ_TILES):
            project(c, k)

        gb = jnp.concatenate([pr_ref[0, rows, H0_GB:H0_GB + GB_HALF],
                              pr_ref[1, rows, H1_GB:H1_GB + GB_HALF]], axis=1).astype(F32)
        og_ref[0, rows, :] = _sigmoid(gb).astype(BF16)

    for c in range(SSD_TILE // CHUNK):
        acts = [conv_strip(c, s) for s in range(N_STRIPS)]
        scan(c, acts, decay_terms(c))

    hnext_ref[...] = _adaln_rmsnorm(x_ref[0], nw_ref[...], scale_ref[...], shift_ref[...])


def _ssd_branch(x, mod, layer, lp, e2):
    b, s, d = x.shape
    tq = SSD_TILE
    assert s % tq == 0
    nt = s // tq
    last = b * nt - 1

    def in_tile(t):
        tt = jnp.minimum(t + 1, last)
        return tt // nt, tt % nt

    def out_tile(t):
        tt = jnp.maximum(t - 1, 0)
        return tt // nt, tt % nt

    batch_in = lambda t: in_tile(t)[0]
    in_specs = [
        pl.BlockSpec((1, tq, d), lambda t: (0, 0, 0), pipeline_mode=pl.Buffered(1)),
        _mod_spec(layer, MOD_SCALE, d, lambda t: 0),
        _mod_spec(layer, MOD_SHIFT, d, lambda t: 0),
        pl.BlockSpec((1, tq, d), lambda t: (*in_tile(t), 0)),
        _resident((1, d), layer),
        _mod_spec(layer, MOD_SCALE, d, batch_in),
        _mod_spec(layer, MOD_SHIFT, d, batch_in),
        *[_w_block(layer, off // W_BLOCK, d) for off in
          (IN_XS, IN_XS + W_BLOCK, IN_B, IN_C, IN_SZ, IN_SZ + W_BLOCK)],
        _w_block(layer, 1, d),
        _resident((d, LANES), layer),
        _resident((SSM_CONV, CONV_DIM), layer), _resident((1, CONV_DIM), layer),
        _resident((1, LANES), layer), _resident((1, LANES), layer),
        _resident((1, SSM_D_INNER), layer), _resident((1, SSM_D_INNER), layer),
        _resident((2 * LANES, SSM_D_INNER)),
    ]
    return pl.pallas_call(
        functools.partial(_ssd_kernel, tiles_per_seq=nt),
        out_shape=(jax.ShapeDtypeStruct((b, s, SSM_D_INNER), BF16),
                   jax.ShapeDtypeStruct((b, s, d), BF16)),
        grid=(b * nt + 1,),
        in_specs=in_specs,
        out_specs=(pl.BlockSpec((1, tq, SSM_D_INNER), lambda t: (*out_tile(t), 0)),
                   pl.BlockSpec((1, tq, d), lambda t: (*out_tile(t), 0))),
        scratch_shapes=[
            pltpu.VMEM((tq, d), BF16),
            pltpu.VMEM((tq, d), BF16),
            pltpu.VMEM((2, tq, HALF_W), BF16),
            pltpu.VMEM((2, tq, HALF_W), BF16),
            pltpu.VMEM((tq, LANES), F32),
            pltpu.VMEM((tq, LANES), F32),
            pltpu.VMEM((SSM_STATE, SSM_D_INNER), F32),
            pltpu.VMEM((TAIL_ROWS, CONV_DIM), BF16),
        ],
        compiler_params=pltpu.CompilerParams(
            dimension_semantics=("arbitrary",),
            vmem_limit_bytes=VMEM_LIMIT_BYTES),
        name="ssd_branch",
    )(x, mod, mod, x, lp["nw"], mod, mod, *([lp["w_bf"]] * 6), lp["w_gates"], lp["w_dt"], lp["convw"], lp["convb"],
      lp["dtb"], lp["alog"], lp["dskip"], lp["nrmw"], e2)


def _gmlp_merge_kernel(x_ref, nw_ref, scale_ref, shift_ref, gate_ref, wuvz_ref, wga_ref, lnw_ref, lnb_ref, ws_ref,
                       bsf_ref, wpa_ref, wpb_ref, wo_ref, yb_ref, gbs_ref, fnw_ref, o_ref, *, tq, final_norm):
    x = x_ref[0]
    hb = _adaln_rmsnorm(x, nw_ref[...], scale_ref[...], shift_ref[...])
    p = jnp.dot(hb, wuvz_ref[...], preferred_element_type=F32)
    ga = jnp.dot(hb, wga_ref[...], preferred_element_type=F32)
    u = _gelu_tanh(p[:, 0:GM_WIDTH])
    v = _gelu_tanh(p[:, GM_WIDTH:2 * GM_WIDTH])
    mu = jnp.mean(v, axis=-1, keepdims=True)
    vc = v - mu
    var = jnp.mean(vc * vc, axis=-1, keepdims=True)
    vn = ((vc * lax.rsqrt(var + EPS)) * lnw_ref[...] + lnb_ref[...]).astype(BF16)

    row = lax.broadcasted_iota(jnp.int32, (CHUNK, CHUNK), 0)
    col = lax.broadcasted_iota(jnp.int32, (CHUNK, CHUNK), 1)
    tril = row >= col
    ws = [jnp.where(tril, ws_ref[g], 0.0).astype(BF16) for g in range(GM_GROUPS)]
    ws_pairs = [jnp.concatenate(ws[2 * i:2 * i + 2], axis=1) for i in range(GM_GROUPS // 2)]
    pair_half = lax.broadcasted_iota(jnp.int32, (CHUNK, 2 * CHUNK), 1) // CHUNK

    def mix_pair(i, k):
        v_pair = vn[k * CHUNK:(k + 1) * CHUNK, 2 * i * CHUNK:(2 * i + 2) * CHUNK]
        v_diag = jnp.concatenate(
            [jnp.where(pair_half == j, v_pair, jnp.zeros_like(v_pair)) for j in range(2)], axis=0)
        return jnp.dot(ws_pairs[i], v_diag, preferred_element_type=F32)

    mixed = jnp.concatenate([
        jnp.concatenate([mix_pair(i, k) for i in range(GM_GROUPS // 2)], axis=1) + bsf_ref[...]
        for k in range(tq // CHUNK)], axis=0)
    ya = ((u * mixed) * _silu(p[:, 2 * GM_WIDTH:3 * GM_WIDTH])).astype(BF16)
    pa = jnp.dot(ya, wpa_ref[...], preferred_element_type=F32)
    pb = jnp.dot(yb_ref[0], wpb_ref[...], preferred_element_type=F32)
    merged = (_sigmoid(ga) * pa + gbs_ref[0].astype(F32) * pb).astype(BF16)
    out = x + gate_ref[...] * jnp.dot(merged, wo_ref[...], preferred_element_type=F32)
    if final_norm:
        out = (out * lax.rsqrt(jnp.mean(out * out, axis=-1, keepdims=True) + EPS)) * fnw_ref[...]
    o_ref[0] = out


def _gmlp_merge(x, mod, layer, lp, yb, gbs, fnw, *, tq, final_norm):
    b, s, d = x.shape
    assert s % tq == 0
    tile = lambda i, j: (i, j, 0)
    batch_of = lambda i, j: i
    in_specs = [
        pl.BlockSpec((1, tq, d), tile),
        _resident((1, d), layer),
        _mod_spec(layer, MOD_SCALE, d, batch_of), _mod_spec(layer, MOD_SHIFT, d, batch_of),
        _mod_spec(layer, MOD_GATE, d, batch_of),
        _w_block(layer, 0, d, width=IN_SZ),
        _w_block(layer, 0, d),
        _resident((1, GM_WIDTH), layer), _resident((1, GM_WIDTH), layer),
        _resident((GM_GROUPS, CHUNK, CHUNK), layer), _resident((CHUNK, GM_WIDTH), layer),
        _resident((GM_WIDTH, d), layer), _resident((SSM_D_INNER, d), layer), _resident((d, d), layer),
        pl.BlockSpec((1, tq, SSM_D_INNER), tile),
        pl.BlockSpec((1, tq, d), tile),
        _resident((1, d)),
    ]
    return pl.pallas_call(
        functools.partial(_gmlp_merge_kernel, tq=tq, final_norm=final_norm),
        out_shape=jax.ShapeDtypeStruct((b, s, d), F32),
        grid=(b, s // tq),
        in_specs=in_specs,
        out_specs=pl.BlockSpec((1, tq, d), tile),
        compiler_params=pltpu.CompilerParams(
            dimension_semantics=("arbitrary", "arbitrary"),
            vmem_limit_bytes=VMEM_LIMIT_BYTES),
        name="gmlp_merge",
    )(x, lp["nw"], mod, mod, mod, lp["w_bf"], lp["w_gates"], lp["lnw"], lp["lnb"], lp["ws"], lp["bsf"],
      lp["wpa"], lp["wpb"], lp["wo"], yb, gbs, fnw)


def _pad_lanes(v):
    return jnp.pad(v, ((0, 0), (0, LANES - v.shape[1])))[:, None, :]


def kernel(x, c, ada_w, ada_b, norm_w, w_in, gm_ln_w, gm_ln_b, gm_ws, gm_bs, conv_w, conv_b, dt_bias, a_log,
           d_skip, ssm_norm_w, w_proj_a, w_proj_b, w_out, final_norm_w):
    depth = w_in.shape[0]
    b, s, d = x.shape
    assert w_in.shape[2] == IN_END

    mod = _adaln_mod(c, ada_w, ada_b).reshape(depth, b, 3, 1, d)
    fnw = final_norm_w.reshape(1, d)
    head_of_chan = jnp.arange(SSM_D_INNER, dtype=jnp.int32) // SSM_HEAD_DIM
    e1 = (jnp.arange(LANES, dtype=jnp.int32)[:, None] == head_of_chan[None, :]).astype(BF16)
    e2 = jnp.concatenate([e1, e1], axis=0)

    lp = dict(
        nw=norm_w[:, None, :],
        w_bf=w_in.astype(BF16),
        w_gates=w_in[:, :, IN_GA:IN_END].astype(BF16),
        w_dt=jnp.pad(w_in[:, :, IN_DT:IN_GA], ((0, 0), (0, 0), (0, LANES - SSM_HEADS))).astype(BF16),
        lnw=gm_ln_w[:, None, :], lnb=gm_ln_b[:, None, :],
        ws=gm_ws,
        bsf=jnp.repeat(jnp.swapaxes(gm_bs, 1, 2), CHUNK, axis=2),
        convw=conv_w, convb=conv_b[:, None, :],
        dtb=_pad_lanes(dt_bias), alog=_pad_lanes(a_log),
        dskip=jnp.repeat(d_skip, SSM_HEAD_DIM, axis=1)[:, None, :],
        nrmw=ssm_norm_w[:, None, :],
        wpa=w_proj_a.astype(BF16), wpb=w_proj_b.astype(BF16), wo=w_out.astype(BF16),
    )
    for l in range(depth):
        yb, gbs = _ssd_branch(x, mod, l, lp, e2)
        x = _gmlp_merge(x, mod, l, lp, yb, gbs, fnw, tq=min(512, s), final_norm=(l == depth - 1))
    return x
```
